```python
import jax, jax.numpy as jnp
from jax import lax
import numpy as np

D_MODEL = 4096
BATCH = 4
SEQ = 4096
DEPTH = 1
DEC_BATCH = 16
DEC_SEQ = 64
PAST_LEN = 4096

CHUNK = 64
D_A = 2048
CONV_W = 31
D_B = 2048
GMLP_CHUNK = 128
GMLP_GROUPS = 16
GMLP_GC = D_B // GMLP_GROUPS
PEER_HEADS = 8
N_KEYS = 128
N_EXPERTS = N_KEYS * N_KEYS
D_KEY = 256
HALF_KEY = D_KEY // 2
TOPK = 16
PEER_BLOCK = 64
EPS = 1e-6
D_IN = 2 * D_A + 2 * D_B + 2 * D_MODEL

kernel_name = "hybrid_conv_gmlp_peer_stream_step"


def rmsnorm(x, g):
    xf = x.astype(jnp.float32)
    r = lax.rsqrt(jnp.mean(xf * xf, axis=-1, keepdims=True) + EPS)
    return (xf * r).astype(x.dtype) * g


def layernorm(x, g, b):
    xf = x.astype(jnp.float32)
    mu = jnp.mean(xf, axis=-1, keepdims=True)
    var = jnp.mean(jnp.square(xf - mu), axis=-1, keepdims=True)
    return ((xf - mu) * lax.rsqrt(var + EPS)).astype(x.dtype) * g + b


def in_projection(xn, w_in):
    h = xn @ w_in
    a_in = h[..., :2 * D_A]
    b_in = h[..., 2 * D_A:2 * D_A + 2 * D_B]
    g_pre = h[..., 2 * D_A + 2 * D_B:]
    return a_in, b_in, g_pre


def glu(a_in):
    return a_in[..., :D_A] * jax.nn.sigmoid(a_in[..., D_A:])


def depthwise_valid(xpad, conv_w, conv_b):
    out = lax.conv_general_dilated(
        xpad, conv_w[:, None, :], window_strides=(1,), padding='VALID',
        dimension_numbers=('NWC', 'WIO', 'NWC'), feature_group_count=D_A)
    return out + conv_b


def conv_tail(c, ln_g, ln_b, w_a_out):
    return jax.nn.silu(layernorm(c, ln_g, ln_b)) @ w_a_out


def gmlp_mask():
    pos = np.arange(GMLP_CHUNK)
    return (pos[:, None] // CHUNK) >= (pos[None, :] // CHUNK)


def gmlp_prompt(b_in, ln_g, ln_b, w_spatial, b_spatial, w_b_out):
    B, T, _ = b_in.shape
    u = b_in[..., :D_B]
    v = layernorm(b_in[..., D_B:], ln_g, ln_b)
    vr = v.reshape(B, T // GMLP_CHUNK, GMLP_CHUNK, GMLP_GROUPS, GMLP_GC)
    ws = jnp.where(gmlp_mask()[None], w_spatial, 0.0)
    z = jnp.einsum('gij,bnjgc->bnigc', ws, vr) + b_spatial.T[None, None, :, :, None]
    return (u * z.reshape(B, T, D_B)) @ w_b_out


def gmlp_sample(b_in, ln_g, ln_b, w_spatial, b_spatial, w_b_out):
    B, T, _ = b_in.shape
    u = b_in[..., :D_B]
    v = layernorm(b_in[..., D_B:], ln_g, ln_b)
    vr = v.reshape(B, T, GMLP_GROUPS, GMLP_GC)
    ws = jnp.where(gmlp_mask()[None, :T, :T], w_spatial[:, :T, :T], 0.0)
    z = jnp.einsum('gij,bjgc->bigc', ws, vr) + b_spatial[:, :T].T[None, :, :, None]
    return (u * z.reshape(B, T, D_B)) @ w_b_out, v


def merge(ya, yb, g_pre, b_gate, w_o):
    g = jax.nn.sigmoid(g_pre + b_gate)
    return (g[..., :D_MODEL] * ya + g[..., D_MODEL:] * yb) @ w_o


def peer(xn, w_query, sub_keys, expert_u, expert_v):
    lead = xn.shape[:-1]
    xf = xn.reshape(-1, D_MODEL)
    n = xf.shape[0]
    q = (xf @ w_query).reshape(n, PEER_HEADS, 2, HALF_KEY)
    s = jnp.einsum('nhpd,hpkd->nhpk', q, sub_keys).astype(jnp.float32)
    sv, si = lax.top_k(s, TOPK)
    comb = sv[:, :, 0, :, None] + sv[:, :, 1, None, :]
    cv, ci = lax.top_k(comb.reshape(n, PEER_HEADS, TOPK * TOPK), TOPK)
    i1 = jnp.take_along_axis(si[:, :, 0], ci // TOPK, axis=-1)
    i2 = jnp.take_along_axis(si[:, :, 1], ci % TOPK, axis=-1)
    eid = i1 * N_KEYS + i2
    gate = jax.nn.softmax(cv, axis=-1)
    nb = -(-n // PEER_BLOCK)
    pad = nb * PEER_BLOCK - n
    xb = jnp.pad(xf, ((0, pad), (0, 0))).reshape(nb, PEER_BLOCK, D_MODEL)
    eb = jnp.pad(eid, ((0, pad), (0, 0), (0, 0))).reshape(nb, PEER_BLOCK, PEER_HEADS, TOPK)
    gb = jnp.pad(gate, ((0, pad), (0, 0), (0, 0))).reshape(nb, PEER_BLOCK, PEER_HEADS, TOPK)

    def block(args):
        xk, ek, gk = args
        u = expert_u[ek]
        h = jax.nn.gelu(jnp.einsum('nhkd,nd->nhk', u, xk), approximate=False)
        w = (gk * h).astype(xk.dtype)
        return jnp.einsum('nhk,nhkd->nd', w, expert_v[ek])

    out = lax.map(block, (xb, eb, gb)).reshape(nb * PEER_BLOCK, D_MODEL)[:n]
    return out.reshape(*lead, D_MODEL)


def setup_inputs(seed: int = 0) -> dict:
    key = jax.random.key(seed)
    ks = jax.random.split(key, 24)
    f32 = jnp.float32

    def nrm(k, shape, scale):
        return jax.random.normal(k, shape, f32) * scale

    L = DEPTH
    return {
        "x_prompt": nrm(ks[0], (BATCH, SEQ, D_MODEL), 1.0),
        "x_sample": nrm(ks[1], (DEC_BATCH, DEC_SEQ, D_MODEL), 1.0),
        "cache_conv": nrm(ks[2], (L, DEC_BATCH, CONV_W - 1, D_A), 0.5),
        "norm_mix_g": 1.0 + nrm(ks[3], (L, D_MODEL), 0.01),
        "w_in": nrm(ks[4], (L, D_MODEL, D_IN), D_MODEL ** -0.5),
        "b_gate": nrm(ks[5], (L, 2 * D_MODEL), 0.1),
        "conv_w": nrm(ks[6], (L, CONV_W, D_A), CONV_W ** -0.5),
        "conv_b": nrm(ks[7], (L, D_A), 0.02),
        "ln_a_g": 1.0 + nrm(ks[8], (L, D_A), 0.01),
        "ln_a_b": nrm(ks[9], (L, D_A), 0.02),
        "w_a_out": nrm(ks[10], (L, D_A, D_MODEL), D_A ** -0.5),
        "ln_b_g": 1.0 + nrm(ks[11], (L, D_B), 0.01),
        "ln_b_b": nrm(ks[12], (L, D_B), 0.02),
        "w_spatial": nrm(ks[13], (L, GMLP_GROUPS, GMLP_CHUNK, GMLP_CHUNK), 0.5 * GMLP_CHUNK ** -0.5),
        "b_spatial": 1.0 + nrm(ks[14], (L, GMLP_GROUPS, GMLP_CHUNK), 0.01),
        "w_b_out": nrm(ks[15], (L, D_B, D_MODEL), D_B ** -0.5),
        "w_o": nrm(ks[16], (L, D_MODEL, D_MODEL), D_MODEL ** -0.5),
        "norm_ffn_g": 1.0 + nrm(ks[17], (L, D_MODEL), 0.01),
        "w_query": nrm(ks[18], (L, D_MODEL, PEER_HEADS * D_KEY), D_MODEL ** -0.5),
        "sub_keys": nrm(ks[19], (L, PEER_HEADS, 2, N_KEYS, HALF_KEY), HALF_KEY ** -0.5),
        "expert_u": nrm(ks[20], (L, N_EXPERTS, D_MODEL), D_MODEL ** -0.5),
        "expert_v": nrm(ks[21], (L, N_EXPERTS, D_MODEL), 0.5),
        "norm_final_g": 1.0 + nrm(ks[22], (D_MODEL,), 0.01),
    }


def reference(x_prompt, x_sample, cache_conv, norm_mix_g, w_in, b_gate, conv_w, conv_b,
              ln_a_g, ln_a_b, w_a_out, ln_b_g, ln_b_b, w_spatial, b_spatial, w_b_out, w_o,
              norm_ffn_g, w_query, sub_keys, expert_u, expert_v, norm_final_g):
    xp, xs = x_prompt, x_sample
    conv_p_list, conv_s_list, v_s_list = [], [], []
    for l in range(DEPTH):
        xn = rmsnorm(xp, norm_mix_g[l])
        a_in, b_in, g_pre = in_projection(xn, w_in[l])
        a = glu(a_in)
        apad = jnp.pad(a, ((0, 0), (CONV_W - 1, 0), (0, 0)))
        ya = conv_tail(depthwise_valid(apad, conv_w[l], conv_b[l]), ln_a_g[l], ln_a_b[l], w_a_out[l])
        yb = gmlp_prompt(b_in, ln_b_g[l], ln_b_b[l], w_spatial[l], b_spatial[l], w_b_out[l])
        xp = xp + merge(ya, yb, g_pre, b_gate[l], w_o[l])
        conv_p_list.append(apad[:, -(CONV_W - 1):])

        xn = rmsnorm(xs, norm_mix_g[l])
        a_in, b_in, g_pre = in_projection(xn, w_in[l])
        hist = jnp.concatenate([cache_conv[l].astype(xs.dtype), glu(a_in)], axis=1)
        ya = conv_tail(depthwise_valid(hist, conv_w[l], conv_b[l]), ln_a_g[l], ln_a_b[l], w_a_out[l])
        yb, v_new = gmlp_sample(b_in, ln_b_g[l], ln_b_b[l], w_spatial[l], b_spatial[l], w_b_out[l])
        xs = xs + merge(ya, yb, g_pre, b_gate[l], w_o[l])
        conv_s_list.append(hist[:, -(CONV_W - 1):])
        v_s_list.append(v_new)

        xp = xp + peer(rmsnorm(xp, norm_ffn_g[l]), w_query[l], sub_keys[l], expert_u[l], expert_v[l])
        xs = xs + peer(rmsnorm(xs, norm_ffn_g[l]), w_query[l], sub_keys[l], expert_u[l], expert_v[l])

    y_prompt = rmsnorm(xp, norm_final_g)
    y_sample = rmsnorm(xs, norm_final_g)
    conv_state_prompt = jnp.stack(conv_p_list, axis=0)
    conv_state_sample = jnp.stack(conv_s_list, axis=0)
    gmlp_v_sample = jnp.stack(v_s_list, axis=0)
    return (y_prompt, y_sample, conv_state_prompt, conv_state_sample, gmlp_v_sample)
```

```python
import functools
import math

import jax
import jax.numpy as jnp
from jax import lax
from jax.experimental import pallas as pl
from jax.experimental.pallas import tpu as pltpu

F32 = jnp.float32
BF16 = jnp.bfloat16

EPS = 1e-6
STREAM_CHUNK = 64
TOPK = 16
SUBLANES = 8
LANES = 128
HALO_ROWS = 32
VMEM_LIMIT_CAP = 60000 * 1024
SQRT_HALF = 1.0 / math.sqrt(2.0)


def _tile(n, pref):
    t = min(n, pref)
    while n % t:
        t //= 2
    return t


def _params(semantics, vmem_bytes):
    limit = int(min(VMEM_LIMIT_CAP, max(vmem_bytes * 5 // 4 + (4 << 20), 16 << 20)))
    return pltpu.CompilerParams(dimension_semantics=semantics, vmem_limit_bytes=limit)


def _sorting_network(n):
    pairs = []
    p = 1
    while p < n:
        k = p
        while k >= 1:
            for j in range(k % p, n - k, 2 * k):
                for i in range(min(k, n - j - k)):
                    if (i + j) // (2 * p) == (i + j + k) // (2 * p):
                        pairs.append((i + j, i + j + k))
            k //= 2
        p *= 2
    return pairs


SORT16 = _sorting_network(TOPK)


def _rmsnorm_kernel(x_ref, g_ref, o_ref):
    x = x_ref[...]
    r = lax.rsqrt(jnp.mean(x * x, axis=-1, keepdims=True) + EPS)
    o_ref[...] = ((x * r) * g_ref[...]).astype(o_ref.dtype)


def rmsnorm_cast(x, g, tm=512):
    n, d = x.shape
    tm = _tile(n, tm)
    return pl.pallas_call(
        _rmsnorm_kernel,
        out_shape=jax.ShapeDtypeStruct((n, d), BF16),
        grid=(n // tm,),
        in_specs=[pl.BlockSpec((tm, d), lambda i: (i, 0)),
                  pl.BlockSpec((1, d), lambda i: (0, 0))],
        out_specs=pl.BlockSpec((tm, d), lambda i: (i, 0)),
        compiler_params=_params(("parallel",), 2 * tm * d * 6),
        name="rmsnorm_cast",
    )(x, g.reshape(1, d))


def _glu_kernel(x_ref, wl_ref, wg_ref, o_ref):
    x = x_ref[...]
    lin = jnp.dot(x, wl_ref[...], preferred_element_type=F32)
    gate = jnp.dot(x, wg_ref[...], preferred_element_type=F32)
    o_ref[...] = lin * jax.nn.sigmoid(gate)


def glu_proj(xn, w, d_a, tm=1024, tn=512):
    n, d = xn.shape
    tm, tn = _tile(n, tm), _tile(d_a, tn)
    nb = d_a // tn
    vmem = 2 * (tm * d * 2 + 2 * d * tn * 2 + tm * tn * 4)
    return pl.pallas_call(
        _glu_kernel,
        out_shape=jax.ShapeDtypeStruct((n, d_a), F32),
        grid=(n // tm, nb),
        in_specs=[pl.BlockSpec((tm, d), lambda i, j: (i, 0)),
                  pl.BlockSpec((d, tn), lambda i, j: (0, j)),
                  pl.BlockSpec((d, tn), lambda i, j: (0, j + nb))],
        out_specs=pl.BlockSpec((tm, tn), lambda i, j: (i, j)),
        compiler_params=_params(("parallel", "arbitrary"), vmem),
        name="glu_proj",
    )(xn, w, w)


def _proj_kernel(x_ref, w_ref, o_ref):
    o_ref[...] = jnp.dot(x_ref[...], w_ref[...], preferred_element_type=F32)


def _proj_gate_kernel(x_ref, w_ref, b_ref, o_ref):
    h = jnp.dot(x_ref[...], w_ref[...], preferred_element_type=F32)
    o_ref[...] = jax.nn.sigmoid(h + b_ref[...])


def proj(xn, w, col0, ncols, bias=None, tm=1024, tn=1024):
    n, d = xn.shape
    tm, tn = _tile(n, tm), _tile(math.gcd(ncols, col0) if col0 else ncols, tn)
    off = col0 // tn
    vmem = 2 * (tm * d * 2 + d * tn * 2 + tm * tn * 4)
    in_specs = [pl.BlockSpec((tm, d), lambda i, j: (i, 0)),
                pl.BlockSpec((d, tn), lambda i, j: (0, j + off))]
    args = [xn, w]
    kern = _proj_kernel
    if bias is not None:
        in_specs.append(pl.BlockSpec((1, tn), lambda i, j: (0, j)))
        args.append(bias.reshape(1, ncols))
        kern = _proj_gate_kernel
    return pl.pallas_call(
        kern,
        out_shape=jax.ShapeDtypeStruct((n, ncols), F32),
        grid=(n // tm, ncols // tn),
        in_specs=in_specs,
        out_specs=pl.BlockSpec((tm, tn), lambda i, j: (i, j)),
        compiler_params=_params(("parallel", "arbitrary"), vmem),
        name="proj_gate" if bias is not None else "proj",
    )(*args)


def _layernorm(x, g, b):
    mu = jnp.mean(x, axis=-1, keepdims=True)
    xc = x - mu
    var = jnp.mean(xc * xc, axis=-1, keepdims=True)
    return (xc * lax.rsqrt(var + EPS)) * g + b


def _conv_kernel(a_ref, prev_ref, hist_ref, cw_ref, cb_ref, g_ref, b_ref, o_ref,
                 buf_ref, cbuf_ref, *, tt, kw, rows, cols):
    t = pl.program_id(1)
    buf_ref[0:HALO_ROWS, :] = jnp.where(t == 0, hist_ref[0], prev_ref[0])
    buf_ref[HALO_ROWS:HALO_ROWS + tt, :] = a_ref[0]
    d_a = a_ref.shape[-1]
    lead = HALO_ROWS - (kw - 1)

    def row_chunk(rc, carry):
        r0 = pl.multiple_of(rc * rows, rows)
        for c in range(d_a // cols):
            cs = slice(c * cols, (c + 1) * cols)
            win = buf_ref[pl.ds(r0, rows + HALO_ROWS), cs]
            acc = jnp.broadcast_to(cb_ref[:, cs], (rows, cols))
            for k in range(kw):
                acc = acc + win[lead + k:lead + k + rows, :] * cw_ref[k:k + 1, cs]
            cbuf_ref[:, cs] = acc
        y = _layernorm(cbuf_ref[...], g_ref[...], b_ref[...])
        o_ref[0, pl.ds(r0, rows), :] = (y * jax.nn.sigmoid(y)).astype(o_ref.dtype)
        return carry

    lax.fori_loop(0, tt // rows, row_chunk, 0)


def conv_ln(a, hist, conv_w, conv_b, ln_g, ln_b, tt=512):
    b, t, d_a = a.shape
    kw = conv_w.shape[0]
    assert kw - 1 <= HALO_ROWS and t % HALO_ROWS == 0
    tt = _tile(t, tt)
    per = tt // HALO_ROWS
    rows = _tile(tt, 64)
    vmem = 2 * (tt * d_a * 4 + 2 * HALO_ROWS * d_a * 4 + tt * d_a * 2) + (tt + HALO_ROWS + rows) * d_a * 4
    vec = lambda v: v.reshape(1, d_a)
    return pl.pallas_call(
        functools.partial(_conv_kernel, tt=tt, kw=kw, rows=rows, cols=LANES),
        out_shape=jax.ShapeDtypeStruct((b, t, d_a), BF16),
        grid=(b, t // tt),
        in_specs=[pl.BlockSpec((1, tt, d_a), lambda i, j: (i, j, 0)),
                  pl.BlockSpec((1, HALO_ROWS, d_a), lambda i, j: (i, jnp.maximum(j * per - 1, 0), 0)),
                  pl.BlockSpec((1, HALO_ROWS, d_a), lambda i, j: (i, 0, 0)),
                  pl.BlockSpec((kw, d_a), lambda i, j: (0, 0)),
                  pl.BlockSpec((1, d_a), lambda i, j: (0, 0)),
                  pl.BlockSpec((1, d_a), lambda i, j: (0, 0)),
                  pl.BlockSpec((1, d_a), lambda i, j: (0, 0))],
        out_specs=pl.BlockSpec((1, tt, d_a), lambda i, j: (i, j, 0)),
        scratch_shapes=[pltpu.VMEM((HALO_ROWS + tt, d_a), F32),
                        pltpu.VMEM((rows, d_a), F32)],
        compiler_params=_params(("parallel", "arbitrary"), vmem),
        name="conv_ln",
    )(a, a, hist, conv_w, vec(conv_b), vec(ln_g), vec(ln_b))


def _gmlp_kernel(uv_ref, g_ref, b_ref, ws_ref, bs_ref, p_ref, v_ref, *, chunk, d_b):
    groups = ws_ref.shape[0]
    gc = d_b // groups
    tg = uv_ref.shape[0]
    v = _layernorm(uv_ref[:, d_b:], g_ref[...], b_ref[...])
    v_ref[...] = v
    row = lax.broadcasted_iota(jnp.int32, (chunk, chunk), 0) // STREAM_CHUNK
    col = lax.broadcasted_iota(jnp.int32, (chunk, chunk), 1) // STREAM_CHUNK
    causal = row >= col
    for g in range(groups):
        ws = jnp.where(causal, ws_ref[g], 0.0).astype(BF16)
        bias = bs_ref[g]
        for c in range(tg // chunk):
            rs = slice(c * chunk, (c + 1) * chunk)
            cs = slice(g * gc, (g + 1) * gc)
            z = jnp.dot(ws, v_ref[rs, cs].astype(BF16), preferred_element_type=F32) + bias
            p_ref[rs, cs] = (uv_ref[rs, cs] * z).astype(p_ref.dtype)


def gmlp_mix(uv, ln_g, ln_b, w_spatial, b_spatial, chunk, tg=512):
    n, two_db = uv.shape
    d_b = two_db // 2
    groups = w_spatial.shape[0]
    ws = w_spatial[:, :chunk, :chunk]
    bs = b_spatial[:, :chunk].reshape(groups, chunk, 1)
    tg = max(_tile(n, tg), chunk)
    assert n % tg == 0 and tg % chunk == 0
    vmem = 2 * (tg * two_db * 4 + tg * d_b * 6) + 2 * groups * chunk * (chunk + LANES) * 4
    return pl.pallas_call(
        functools.partial(_gmlp_kernel, chunk=chunk, d_b=d_b),
        out_shape=(jax.ShapeDtypeStruct((n, d_b), BF16), jax.ShapeDtypeStruct((n, d_b), F32)),
        grid=(n // tg,),
        in_specs=[pl.BlockSpec((tg, two_db), lambda i: (i, 0)),
                  pl.BlockSpec((1, d_b), lambda i: (0, 0)),
                  pl.BlockSpec((1, d_b), lambda i: (0, 0)),
                  pl.BlockSpec((groups, chunk, chunk), lambda i: (0, 0, 0)),
                  pl.BlockSpec((groups, chunk, 1), lambda i: (0, 0, 0))],
        out_specs=(pl.BlockSpec((tg, d_b), lambda i: (i, 0)),
                   pl.BlockSpec((tg, d_b), lambda i: (i, 0))),
        compiler_params=_params(("parallel",), vmem),
        name="gmlp_mix",
    )(uv, ln_g.reshape(1, d_b), ln_b.reshape(1, d_b), ws, bs)


def _merge_kernel(act_ref, p_ref, wa_ref, wb_ref, ga_ref, gb_ref, o_ref):
    ya = jnp.dot(act_ref[...], wa_ref[...], preferred_element_type=F32)
    yb = jnp.dot(p_ref[...], wb_ref[...], preferred_element_type=F32)
    o_ref[...] = (ga_ref[...] * ya + gb_ref[...] * yb).astype(o_ref.dtype)


def merge_proj(act, p, gates, wa, wb, tm=1024, tn=512):
    n, d_a = act.shape
    d_b = p.shape[1]
    d = wa.shape[1]
    tm, tn = _tile(n, tm), _tile(d, tn)
    nb = d // tn
    vmem = 2 * (tm * (d_a + d_b) * 2 + (d_a + d_b) * tn * 2 + 2 * tm * tn * 4 + tm * tn * 2)
    return pl.pallas_call(
        _merge_kernel,
        out_shape=jax.ShapeDtypeStruct((n, d), BF16),
        grid=(n // tm, nb),
        in_specs=[pl.BlockSpec((tm, d_a), lambda i, j: (i, 0)),
                  pl.BlockSpec((tm, d_b), lambda i, j: (i, 0)),
                  pl.BlockSpec((d_a, tn), lambda i, j: (0, j)),
                  pl.BlockSpec((d_b, tn), lambda i, j: (0, j)),
                  pl.BlockSpec((tm, tn), lambda i, j: (i, j)),
                  pl.BlockSpec((tm, tn), lambda i, j: (i, j + nb))],
        out_specs=pl.BlockSpec((tm, tn), lambda i, j: (i, j)),
        compiler_params=_params(("parallel", "arbitrary"), vmem),
        name="merge_proj",
    )(act, p, wa, wb, gates, gates)


def _residual_proj_kernel(m_ref, w_ref, x_ref, o_ref):
    o_ref[...] = x_ref[...] + jnp.dot(m_ref[...], w_ref[...], preferred_element_type=F32)


def residual_proj(m, w, x, tm=1024, tn=512):
    n, k = m.shape
    d = w.shape[1]
    tm, tn = _tile(n, tm), _tile(d, tn)
    vmem = 2 * (tm * k * 2 + k * tn * 2 + 2 * tm * tn * 4)
    return pl.pallas_call(
        _residual_proj_kernel,
        out_shape=jax.ShapeDtypeStruct((n, d), F32),
        grid=(n // tm, d // tn),
        in_specs=[pl.BlockSpec((tm, k), lambda i, j: (i, 0)),
                  pl.BlockSpec((k, tn), lambda i, j: (0, j)),
                  pl.BlockSpec((tm, tn), lambda i, j: (i, j))],
        out_specs=pl.BlockSpec((tm, tn), lambda i, j: (i, j)),
        compiler_params=_params(("parallel", "arbitrary"), vmem),
        name="residual_proj",
    )(m, w, x)


def _norm_query_kernel(x_ref, g_ref, wq_ref, xt_ref, qt_ref):
    @pl.when(pl.program_id(1) == 0)
    def _():
        x = x_ref[...]
        r = lax.rsqrt(jnp.mean(x * x, axis=-1, keepdims=True) + EPS)
        xt_ref[...] = ((x * r) * g_ref[...]).T.astype(xt_ref.dtype)

    qt_ref[...] = jnp.dot(wq_ref[...], xt_ref[...], preferred_element_type=F32).astype(qt_ref.dtype)


def norm_query(x, g, wq_t, tm=512, tq=512):
    n, d = x.shape
    qd = wq_t.shape[0]
    tm, tq = _tile(n, tm), _tile(qd, tq)
    vmem = 2 * (tm * d * 4 + tq * d * 2 + d * tm * 2 + tq * tm * 2) + tm * d * 4
    return pl.pallas_call(
        _norm_query_kernel,
        out_shape=(jax.ShapeDtypeStruct((d, n), BF16), jax.ShapeDtypeStruct((qd, n), BF16)),
        grid=(n // tm, qd // tq),
        in_specs=[pl.BlockSpec((tm, d), lambda i, j: (i, 0)),
                  pl.BlockSpec((1, d), lambda i, j: (0, 0)),
                  pl.BlockSpec((tq, d), lambda i, j: (j, 0))],
        out_specs=(pl.BlockSpec((d, tm), lambda i, j: (0, i)),
                   pl.BlockSpec((tq, tm), lambda i, j: (j, i))),
        compiler_params=_params(("parallel", "arbitrary"), vmem),
        name="norm_query",
    )(x, g.reshape(1, d), wq_t)


def _compare_exchange(v, i, j):
    hi, lo = jnp.maximum(v[i], v[j]), jnp.minimum(v[i], v[j])
    v[i], v[j] = hi, lo


def _bitonic_merge(v):
    d = TOPK // 2
    while d >= 1:
        for i in range(TOPK):
            if not i & d:
                _compare_exchange(v, i, i + d)
        d //= 2


def _merge_top(top, other):
    out = list(top)
    for i, x in enumerate(other):
        out[TOPK - 1 - i] = jnp.maximum(top[TOPK - 1 - i], x)
    _bitonic_merge(out)
    return out


def _top_values(s):
    nk = s.shape[0]
    per_sublane = [s[SUBLANES * i:SUBLANES * (i + 1), :] for i in range(nk // SUBLANES)]
    top = None
    for base in range(0, len(per_sublane), TOPK):
        v = per_sublane[base:base + TOPK]
        for i, j in SORT16:
            _compare_exchange(v, i, j)
        top = v if top is None else _merge_top(top, v)
    shift = SUBLANES // 2
    while shift >= 1:
        top = _merge_top(top, [pltpu.roll(x, shift, 0) for x in top])
        shift //= 2
    return top


def _pair_top_values(a, b):
    split = 4
    top = [a[0] + b[j] for j in range(TOPK)]
    for i in range(1, split):
        top = _merge_top(top, [a[i] + b[j] for j in range(TOPK // (i + 1))])
    for j in range(TOPK // (split + 1)):
        top = _merge_top(top, [a[i] + b[j] for i in range(split, TOPK) if (i + 1) * (j + 1) <= TOPK])
    return top


def _topk_kernel(q_ref, k_ref, s1_ref, s2_ref, thr_ref, cst_ref, *, heads, nk, hk):
    sublane = lax.broadcasted_iota(jnp.int32, thr_ref.shape, 0)
    thr = jnp.zeros(thr_ref.shape, F32)
    cst = jnp.zeros(cst_ref.shape, F32)
    for h in range(heads):
        vals = []
        for p, s_ref in enumerate((s1_ref, s2_ref)):
            q = q_ref[(2 * h + p) * hk:(2 * h + p + 1) * hk, :]
            s = jnp.dot(k_ref[h, p], q, preferred_element_type=F32)
            s_ref[h * nk:(h + 1) * nk, :] = s
            vals.append(_top_values(s))
        top = _pair_top_values(vals[0], vals[1])
        z = jnp.exp(top[0] - top[0])
        for k in range(1, TOPK):
            z = z + jnp.exp(top[k] - top[0])
        thr = jnp.where(sublane == h, top[TOPK - 1], thr)
        cst = jnp.where(sublane == h, top[0] + jnp.log(z), cst)
    thr_ref[...] = thr
    cst_ref[...] = cst


def peer_topk(q_t, keys, tn=256):
    qd, n = q_t.shape
    heads, _, nk, hk = keys.shape
    assert nk % (SUBLANES * TOPK) == 0 and heads == SUBLANES
    tn = _tile(n, tn)
    vmem = 2 * (qd * tn * 2 + keys.size * 2 + 2 * heads * nk * tn * 4 + 2 * SUBLANES * tn * 4)
    return pl.pallas_call(
        functools.partial(_topk_kernel, heads=heads, nk=nk, hk=hk),
        out_shape=(jax.ShapeDtypeStruct((heads * nk, n), F32),
                   jax.ShapeDtypeStruct((heads * nk, n), F32),
                   jax.ShapeDtypeStruct((heads, n), F32),
                   jax.ShapeDtypeStruct((heads, n), F32)),
        grid=(n // tn,),
        in_specs=[pl.BlockSpec((qd, tn), lambda i: (0, i)),
                  pl.BlockSpec((heads, 2, nk, hk), lambda i: (0, 0, 0, 0))],
        out_specs=(pl.BlockSpec((heads * nk, tn), lambda i: (0, i)),
                   pl.BlockSpec((heads * nk, tn), lambda i: (0, i)),
                   pl.BlockSpec((heads, tn), lambda i: (0, i)),
                   pl.BlockSpec((heads, tn), lambda i: (0, i))),
        compiler_params=_params(("parallel",), vmem),
        name="peer_topk",
    )(q_t, keys)


def _peer_kernel(xt_ref, s1_ref, s2_ref, thr_ref, cst_ref, u_ref, vt_ref, o_ref,
                 h_ref, w_ref, *, heads, nk, lane_chunk):
    j = pl.program_id(1)
    te, tm = h_ref.shape

    @pl.when(j == 0)
    def _():
        o_ref[...] = jnp.zeros_like(o_ref)

    h_ref[...] = jnp.dot(u_ref[...], xt_ref[...], preferred_element_type=F32)
    for lc in range(tm // lane_chunk):
        ls = slice(lc * lane_chunk, (lc + 1) * lane_chunk)
        for ii in range(te // nk):
            row1 = j * (te // nk) + ii
            gate = None
            for h in range(heads):
                s = s2_ref[h * nk:(h + 1) * nk, ls] + s1_ref[pl.ds(h * nk + row1, 1), ls]
                keep = s >= thr_ref[h:h + 1, ls]
                term = jnp.where(keep, jnp.exp(s - cst_ref[h:h + 1, ls]), 0.0)
                gate = term if gate is None else gate + term
            hh = h_ref[ii * nk:(ii + 1) * nk, ls]
            gelu = 0.5 * hh * (1.0 + lax.erf(hh * SQRT_HALF))
            w_ref[ii * nk:(ii + 1) * nk, ls] = (gate * gelu).astype(w_ref.dtype)
    o_ref[...] += jnp.dot(vt_ref[...], w_ref[...], preferred_element_type=F32)


def peer_dense(xn_t, s1_t, s2_t, thr, cst, u, v_t, tm=512, te=512):
    d, n = xn_t.shape
    ne = u.shape[0]
    heads = thr.shape[0]
    nk = s1_t.shape[0] // heads
    assert ne == nk * nk
    tm, te = _tile(n, tm), max(_tile(ne, te), nk)
    vmem = 2 * (d * tm * 2 + 2 * heads * nk * tm * 4 + 2 * SUBLANES * tm * 4 + 2 * te * d * 2 + d * tm * 4) + te * tm * 6
    return pl.pallas_call(
        functools.partial(_peer_kernel, heads=heads, nk=nk, lane_chunk=_tile(tm, 2 * LANES)),
        out_shape=jax.ShapeDtypeStruct((d, n), F32),
        grid=(n // tm, ne // te),
        in_specs=[pl.BlockSpec((d, tm), lambda i, j: (0, i)),
                  pl.BlockSpec((heads * nk, tm), lambda i, j: (0, i)),
                  pl.BlockSpec((heads * nk, tm), lambda i, j: (0, i)),
                  pl.BlockSpec((heads, tm), lambda i, j: (0, i)),
                  pl.BlockSpec((heads, tm), lambda i, j: (0, i)),
                  pl.BlockSpec((te, d), lambda i, j: (j, 0)),
                  pl.BlockSpec((d, te), lambda i, j: (0, j))],
        out_specs=pl.BlockSpec((d, tm), lambda i, j: (0, i)),
        scratch_shapes=[pltpu.VMEM((te, tm), F32), pltpu.VMEM((te, tm), BF16)],
        compiler_params=_params(("parallel", "arbitrary"), vmem),
        name="peer_dense",
    )(xn_t, s1_t, s2_t, thr, cst, u, v_t)


def _final_kernel(x_ref, pt_ref, g_ref, o_ref):
    x = x_ref[...] + pt_ref[...].T
    r = lax.rsqrt(jnp.mean(x * x, axis=-1, keepdims=True) + EPS)
    o_ref[...] = (x * r) * g_ref[...]


def residual_rmsnorm_t(x, p_t, g, tm=256):
    n, d = x.shape
    tm = _tile(n, tm)
    return pl.pallas_call(
        _final_kernel,
        out_shape=jax.ShapeDtypeStruct((n, d), F32),
        grid=(n // tm,),
        in_specs=[pl.BlockSpec((tm, d), lambda i: (i, 0)),
                  pl.BlockSpec((d, tm), lambda i: (0, i)),
                  pl.BlockSpec((1, d), lambda i: (0, 0))],
        out_specs=pl.BlockSpec((tm, d), lambda i: (i, 0)),
        compiler_params=_params(("parallel",), 2 * 3 * tm * d * 4 + tm * d * 4),
        name="residual_rmsnorm_t",
    )(x, p_t, g.reshape(1, d))


def _residual_t_kernel(x_ref, pt_ref, o_ref):
    o_ref[...] = x_ref[...] + pt_ref[...].T


def residual_t(x, p_t, tm=256):
    n, d = x.shape
    tm = _tile(n, tm)
    return pl.pallas_call(
        _residual_t_kernel,
        out_shape=jax.ShapeDtypeStruct((n, d), F32),
        grid=(n // tm,),
        in_specs=[pl.BlockSpec((tm, d), lambda i: (i, 0)),
                  pl.BlockSpec((d, tm), lambda i: (0, i))],
        out_specs=pl.BlockSpec((tm, d), lambda i: (i, 0)),
        compiler_params=_params(("parallel",), 2 * 3 * tm * d * 4 + tm * d * 4),
        name="residual_t",
    )(x, p_t)


def _layer(x, hist, lw, chunk):
    b, t, d = x.shape
    n = b * t
    d_a = lw["conv_w"].shape[1]
    d_b = lw["ln_b_g"].shape[0]
    x2 = x.reshape(n, d)
    xn = rmsnorm_cast(x2, lw["norm_mix_g"])
    a = glu_proj(xn, lw["w_in"], d_a)
    uv = proj(xn, lw["w_in"], 2 * d_a, 2 * d_b)
    gates = proj(xn, lw["w_in"], 2 * d_a + 2 * d_b, 2 * d, bias=lw["b_gate"])
    act = conv_ln(a.reshape(b, t, d_a), hist, lw["conv_w"], lw["conv_b"], lw["ln_a_g"], lw["ln_a_b"])
    p, v = gmlp_mix(uv, lw["ln_b_g"], lw["ln_b_b"], lw["w_spatial"], lw["b_spatial"], chunk)
    m = merge_proj(act.reshape(n, d_a), p, gates, lw["w_a_out"], lw["w_b_out"])
    x1 = residual_proj(m, lw["w_o"], x2)
    xn_t, q_t = norm_query(x1, lw["norm_ffn_g"], lw["w_query_t"])
    s1_t, s2_t, thr, cst = peer_topk(q_t, lw["sub_keys"])
    peer_t = peer_dense(xn_t, s1_t, s2_t, thr, cst, lw["expert_u"], lw["expert_v_t"])
    return x1, peer_t, a.reshape(b, t, d_a), v.reshape(b, t, d_b)


def kernel(x_prompt, x_sample, cache_conv, norm_mix_g, w_in, b_gate, conv_w, conv_b, ln_a_g, ln_a_b, w_a_out, ln_b_g, ln_b_b, w_spatial, b_spatial, w_b_out, w_o, norm_ffn_g, w_query, sub_keys, expert_u, expert_v, norm_final_g):
    depth = w_in.shape[0]
    kw = conv_w.shape[1]
    d_a = conv_w.shape[2]
    gmlp_chunk = w_spatial.shape[2]
    xp, xs = x_prompt, x_sample
    bp, tp, d = xp.shape
    bs, ts, _ = xs.shape
    assert tp % gmlp_chunk == 0 and ts <= gmlp_chunk and ts % STREAM_CHUNK == 0
    assert tp >= kw - 1 and ts >= kw - 1
    conv_p, conv_s, v_s = [], [], []
    for l in range(depth):
        lw = dict(
            norm_mix_g=norm_mix_g[l], w_in=w_in[l].astype(BF16), b_gate=b_gate[l],
            conv_w=conv_w[l], conv_b=conv_b[l], ln_a_g=ln_a_g[l], ln_a_b=ln_a_b[l],
            w_a_out=w_a_out[l].astype(BF16), ln_b_g=ln_b_g[l], ln_b_b=ln_b_b[l],
            w_spatial=w_spatial[l], b_spatial=b_spatial[l], w_b_out=w_b_out[l].astype(BF16),
            w_o=w_o[l].astype(BF16), norm_ffn_g=norm_ffn_g[l],
            w_query_t=w_query[l].T.astype(BF16), sub_keys=sub_keys[l].astype(BF16),
            expert_u=expert_u[l].astype(BF16), expert_v_t=expert_v[l].T.astype(BF16))
        last = l == depth - 1
        hist_p = jnp.zeros((bp, HALO_ROWS, d_a), F32)
        hist_s = jnp.pad(cache_conv[l].astype(F32), ((0, 0), (HALO_ROWS - (kw - 1), 0), (0, 0)))
        outs = []
        for x, hist, chunk in ((xp, hist_p, gmlp_chunk), (xs, hist_s, ts)):
            b, t, _ = x.shape
            x1, peer_t, a, v = _layer(x, hist, lw, chunk)
            if last:
                y = residual_rmsnorm_t(x1, peer_t, norm_final_g)
            else:
                y = residual_t(x1, peer_t)
            outs.append((y.reshape(b, t, d), a, v))
        (xp, a_p, _), (xs, a_s, v_new) = outs
        conv_p.append(a_p[:, tp - (kw - 1):])
        conv_s.append(a_s[:, ts - (kw - 1):])
        v_s.append(v_new)
    return (xp, xs, jnp.stack(conv_p, axis=0), jnp.stack(conv_s, axis=0), jnp.stack(v_s, axis=0))
```

```python
import functools
import math

import jax
import jax.numpy as jnp
from jax import lax
from jax.experimental import pallas as pl
from jax.experimental.pallas import tpu as pltpu

F32 = jnp.float32
BF16 = jnp.bfloat16

EPS = 1e-6
STREAM_CHUNK = 64
TOPK = 16
SUBLANES = 8
LANES = 128
HALO_ROWS = 32
VMEM_LIMIT_CAP = 60000 * 1024
SQRT_HALF = 1.0 / math.sqrt(2.0)


def _tile(n, pref):
    t = min(n, pref)
    while n % t:
        t //= 2
    return t


def _params(semantics, vmem_bytes):
    limit = int(min(VMEM_LIMIT_CAP, max(vmem_bytes * 5 // 4 + (4 << 20), 16 << 20)))
    return pltpu.CompilerParams(dimension_semantics=semantics, vmem_limit_bytes=limit)


def _sorting_network(n):
    pairs = []
    p = 1
    while p < n:
        k = p
        while k >= 1:
            for j in range(k % p, n - k, 2 * k):
                for i in range(min(k, n - j - k)):
                    if (i + j) // (2 * p) == (i + j + k) // (2 * p):
                        pairs.append((i + j, i + j + k))
            k //= 2
        p *= 2
    return pairs


SORT16 = _sorting_network(TOPK)


def _rmsnorm_kernel(x_ref, g_ref, o_ref):
    x = x_ref[...]
    r = lax.rsqrt(jnp.mean(x * x, axis=-1, keepdims=True) + EPS)
    o_ref[...] = ((x * r) * g_ref[...]).astype(o_ref.dtype)


def rmsnorm_cast(x, g, tm=512):
    n, d = x.shape
    tm = _tile(n, tm)
    return pl.pallas_call(
        _rmsnorm_kernel,
        out_shape=jax.ShapeDtypeStruct((n, d), BF16),
        grid=(n // tm,),
        in_specs=[pl.BlockSpec((tm, d), lambda i: (i, 0)),
                  pl.BlockSpec((1, d), lambda i: (0, 0))],
        out_specs=pl.BlockSpec((tm, d), lambda i: (i, 0)),
        compiler_params=_params(("parallel",), 2 * tm * d * 6),
        name="rmsnorm_cast",
    )(x, g.reshape(1, d))


def _glu_kernel(x_ref, wl_ref, wg_ref, o_ref):
    x = x_ref[...]
    lin = jnp.dot(x, wl_ref[...], preferred_element_type=F32)
    gate = jnp.dot(x, wg_ref[...], preferred_element_type=F32)
    o_ref[...] = lin * jax.nn.sigmoid(gate)


def glu_proj(xn, w, d_a, tm=1024, tn=512):
    n, d = xn.shape
    tm, tn = _tile(n, tm), _tile(d_a, tn)
    nb = d_a // tn
    vmem = 2 * (tm * d * 2 + 2 * d * tn * 2 + tm * tn * 4)
    return pl.pallas_call(
        _glu_kernel,
        out_shape=jax.ShapeDtypeStruct((n, d_a), F32),
        grid=(n // tm, nb),
        in_specs=[pl.BlockSpec((tm, d), lambda i, j: (i, 0)),
                  pl.BlockSpec((d, tn), lambda i, j: (0, j)),
                  pl.BlockSpec((d, tn), lambda i, j: (0, j + nb))],
        out_specs=pl.BlockSpec((tm, tn), lambda i, j: (i, j)),
        compiler_params=_params(("parallel", "arbitrary"), vmem),
        name="glu_proj",
    )(xn, w, w)


def _proj_kernel(x_ref, w_ref, o_ref):
    o_ref[...] = jnp.dot(x_ref[...], w_ref[...], preferred_element_type=F32)


def _proj_gate_kernel(x_ref, w_ref, b_ref, o_ref):
    h = jnp.dot(x_ref[...], w_ref[...], preferred_element_type=F32)
    o_ref[...] = jax.nn.sigmoid(h + b_ref[...])


def proj(xn, w, col0, ncols, bias=None, tm=1024, tn=1024):
    n, d = xn.shape
    tm, tn = _tile(n, tm), _tile(math.gcd(ncols, col0) if col0 else ncols, tn)
    off = col0 // tn
    vmem = 2 * (tm * d * 2 + d * tn * 2 + tm * tn * 4)
    in_specs = [pl.BlockSpec((tm, d), lambda i, j: (i, 0)),
                pl.BlockSpec((d, tn), lambda i, j: (0, j + off))]
    args = [xn, w]
    kern = _proj_kernel
    if bias is not None:
        in_specs.append(pl.BlockSpec((1, tn), lambda i, j: (0, j)))
        args.append(bias.reshape(1, ncols))
        kern = _proj_gate_kernel
    return pl.pallas_call(
        kern,
        out_shape=jax.ShapeDtypeStruct((n, ncols), F32),
        grid=(n // tm, ncols // tn),
        in_specs=in_specs,
        out_specs=pl.BlockSpec((tm, tn), lambda i, j: (i, j)),
        compiler_params=_params(("parallel", "arbitrary"), vmem),
        name="proj_gate" if bias is not None else "proj",
    )(*args)


def _layernorm(x, g, b):
    mu = jnp.mean(x, axis=-1, keepdims=True)
    xc = x - mu
    var = jnp.mean(xc * xc, axis=-1, keepdims=True)
    return (xc * lax.rsqrt(var + EPS)) * g + b


def _conv_kernel(a_ref, prev_ref, hist_ref, cw_ref, cb_ref, g_ref, b_ref, o_ref,
                 buf_ref, cbuf_ref, *, tt, kw, rows, cols):
    t = pl.program_id(1)
    buf_ref[0:HALO_ROWS, :] = jnp.where(t == 0, hist_ref[0], prev_ref[0])
    buf_ref[HALO_ROWS:HALO_ROWS + tt, :] = a_ref[0]
    d_a = a_ref.shape[-1]
    lead = HALO_ROWS - (kw - 1)

    span = rows + HALO_ROWS

    def row_chunk(rc, carry):
        r0 = pl.multiple_of(rc * rows, rows)
        for c in range(d_a // cols):
            cs = slice(c * cols, (c + 1) * cols)
            win = buf_ref[pl.ds(r0, span), cs]
            acc = jnp.broadcast_to(cb_ref[:, cs], (rows, cols))
            for s in range(SUBLANES):
                shifted = win if s == 0 else pltpu.roll(win, span - s, 0)
                for off in range(s, HALO_ROWS + 1, SUBLANES):
                    k = off - lead
                    if 0 <= k < kw:
                        acc = acc + shifted[off - s:off - s + rows, :] * cw_ref[k:k + 1, cs]
            cbuf_ref[:, cs] = acc
        y = _layernorm(cbuf_ref[...], g_ref[...], b_ref[...])
        o_ref[0, pl.ds(r0, rows), :] = (y * jax.nn.sigmoid(y)).astype(o_ref.dtype)
        return carry

    lax.fori_loop(0, tt // rows, row_chunk, 0)


def conv_ln(a, hist, conv_w, conv_b, ln_g, ln_b, tt=512):
    b, t, d_a = a.shape
    kw = conv_w.shape[0]
    assert kw - 1 <= HALO_ROWS and t % HALO_ROWS == 0
    tt = _tile(t, tt)
    per = tt // HALO_ROWS
    rows = _tile(tt, 64)
    vmem = 2 * (tt * d_a * 4 + 2 * HALO_ROWS * d_a * 4 + tt * d_a * 2) + (tt + HALO_ROWS + rows) * d_a * 4
    vec = lambda v: v.reshape(1, d_a)
    return pl.pallas_call(
        functools.partial(_conv_kernel, tt=tt, kw=kw, rows=rows, cols=LANES),
        out_shape=jax.ShapeDtypeStruct((b, t, d_a), BF16),
        grid=(b, t // tt),
        in_specs=[pl.BlockSpec((1, tt, d_a), lambda i, j: (i, j, 0)),
                  pl.BlockSpec((1, HALO_ROWS, d_a), lambda i, j: (i, jnp.maximum(j * per - 1, 0), 0)),
                  pl.BlockSpec((1, HALO_ROWS, d_a), lambda i, j: (i, 0, 0)),
                  pl.BlockSpec((kw, d_a), lambda i, j: (0, 0)),
                  pl.BlockSpec((1, d_a), lambda i, j: (0, 0)),
                  pl.BlockSpec((1, d_a), lambda i, j: (0, 0)),
                  pl.BlockSpec((1, d_a), lambda i, j: (0, 0))],
        out_specs=pl.BlockSpec((1, tt, d_a), lambda i, j: (i, j, 0)),
        scratch_shapes=[pltpu.VMEM((HALO_ROWS + tt, d_a), F32),
                        pltpu.VMEM((rows, d_a), F32)],
        compiler_params=_params(("parallel", "arbitrary"), vmem),
        name="conv_ln",
    )(a, a, hist, conv_w, vec(conv_b), vec(ln_g), vec(ln_b))


def _gmlp_kernel(uv_ref, g_ref, b_ref, ws_ref, bs_ref, p_ref, v_ref, *, chunk, d_b):
    groups = ws_ref.shape[0]
    gc = d_b // groups
    tg = uv_ref.shape[0]
    v = _layernorm(uv_ref[:, d_b:], g_ref[...], b_ref[...])
    v_ref[...] = v
    row = lax.broadcasted_iota(jnp.int32, (chunk, chunk), 0) // STREAM_CHUNK
    col = lax.broadcasted_iota(jnp.int32, (chunk, chunk), 1) // STREAM_CHUNK
    causal = row >= col
    for g in range(groups):
        ws = jnp.where(causal, ws_ref[g], 0.0).astype(BF16)
        bias = bs_ref[g]
        for c in range(tg // chunk):
            rs = slice(c * chunk, (c + 1) * chunk)
            cs = slice(g * gc, (g + 1) * gc)
            z = jnp.dot(ws, v_ref[rs, cs].astype(BF16), preferred_element_type=F32) + bias
            p_ref[rs, cs] = (uv_ref[rs, cs] * z).astype(p_ref.dtype)


def gmlp_mix(uv, ln_g, ln_b, w_spatial, b_spatial, chunk, tg=512):
    n, two_db = uv.shape
    d_b = two_db // 2
    groups = w_spatial.shape[0]
    ws = w_spatial[:, :chunk, :chunk]
    bs = b_spatial[:, :chunk].reshape(groups, chunk, 1)
    tg = max(_tile(n, tg), chunk)
    assert n % tg == 0 and tg % chunk == 0
    vmem = 2 * (tg * two_db * 4 + tg * d_b * 6) + 2 * groups * chunk * (chunk + LANES) * 4
    return pl.pallas_call(
        functools.partial(_gmlp_kernel, chunk=chunk, d_b=d_b),
        out_shape=(jax.ShapeDtypeStruct((n, d_b), BF16), jax.ShapeDtypeStruct((n, d_b), F32)),
        grid=(n // tg,),
        in_specs=[pl.BlockSpec((tg, two_db), lambda i: (i, 0)),
                  pl.BlockSpec((1, d_b), lambda i: (0, 0)),
                  pl.BlockSpec((1, d_b), lambda i: (0, 0)),
                  pl.BlockSpec((groups, chunk, chunk), lambda i: (0, 0, 0)),
                  pl.BlockSpec((groups, chunk, 1), lambda i: (0, 0, 0))],
        out_specs=(pl.BlockSpec((tg, d_b), lambda i: (i, 0)),
                   pl.BlockSpec((tg, d_b), lambda i: (i, 0))),
        compiler_params=_params(("parallel",), vmem),
        name="gmlp_mix",
    )(uv, ln_g.reshape(1, d_b), ln_b.reshape(1, d_b), ws, bs)


def _merge_kernel(act_ref, p_ref, wa_ref, wb_ref, ga_ref, gb_ref, o_ref):
    ya = jnp.dot(act_ref[...], wa_ref[...], preferred_element_type=F32)
    yb = jnp.dot(p_ref[...], wb_ref[...], preferred_element_type=F32)
    o_ref[...] = (ga_ref[...] * ya + gb_ref[...] * yb).astype(o_ref.dtype)


def merge_proj(act, p, gates, wa, wb, tm=1024, tn=512):
    n, d_a = act.shape
    d_b = p.shape[1]
    d = wa.shape[1]
    tm, tn = _tile(n, tm), _tile(d, tn)
    nb = d // tn
    vmem = 2 * (tm * (d_a + d_b) * 2 + (d_a + d_b) * tn * 2 + 2 * tm * tn * 4 + tm * tn * 2)
    return pl.pallas_call(
        _merge_kernel,
        out_shape=jax.ShapeDtypeStruct((n, d), BF16),
        grid=(n // tm, nb),
        in_specs=[pl.BlockSpec((tm, d_a), lambda i, j: (i, 0)),
                  pl.BlockSpec((tm, d_b), lambda i, j: (i, 0)),
                  pl.BlockSpec((d_a, tn), lambda i, j: (0, j)),
                  pl.BlockSpec((d_b, tn), lambda i, j: (0, j)),
                  pl.BlockSpec((tm, tn), lambda i, j: (i, j)),
                  pl.BlockSpec((tm, tn), lambda i, j: (i, j + nb))],
        out_specs=pl.BlockSpec((tm, tn), lambda i, j: (i, j)),
        compiler_params=_params(("parallel", "arbitrary"), vmem),
        name="merge_proj",
    )(act, p, wa, wb, gates, gates)


def _residual_proj_kernel(m_ref, w_ref, x_ref, o_ref):
    o_ref[...] = x_ref[...] + jnp.dot(m_ref[...], w_ref[...], preferred_element_type=F32)


def residual_proj(m, w, x, tm=1024, tn=512):
    n, k = m.shape
    d = w.shape[1]
    tm, tn = _tile(n, tm), _tile(d, tn)
    vmem = 2 * (tm * k * 2 + k * tn * 2 + 2 * tm * tn * 4)
    return pl.pallas_call(
        _residual_proj_kernel,
        out_shape=jax.ShapeDtypeStruct((n, d), F32),
        grid=(n // tm, d // tn),
        in_specs=[pl.BlockSpec((tm, k), lambda i, j: (i, 0)),
                  pl.BlockSpec((k, tn), lambda i, j: (0, j)),
                  pl.BlockSpec((tm, tn), lambda i, j: (i, j))],
        out_specs=pl.BlockSpec((tm, tn), lambda i, j: (i, j)),
        compiler_params=_params(("parallel", "arbitrary"), vmem),
        name="residual_proj",
    )(m, w, x)


def _norm_query_kernel(x_ref, g_ref, wq_ref, xt_ref, qt_ref):
    x = x_ref[...]
    r = lax.rsqrt(jnp.mean(x * x, axis=-1, keepdims=True) + EPS)
    xt_ref[...] = ((x * r) * g_ref[...]).T.astype(xt_ref.dtype)
    qt_ref[...] = jnp.dot(wq_ref[...], xt_ref[...], preferred_element_type=F32).astype(qt_ref.dtype)


def norm_query(x, g, wq_t, tm=512):
    n, d = x.shape
    qd = wq_t.shape[0]
    tm = _tile(n, tm)
    vmem = 2 * (tm * d * 4 + d * tm * 2 + qd * tm * 2) + qd * d * 2 + tm * d * 8
    return pl.pallas_call(
        _norm_query_kernel,
        out_shape=(jax.ShapeDtypeStruct((d, n), BF16), jax.ShapeDtypeStruct((qd, n), BF16)),
        grid=(n // tm,),
        in_specs=[pl.BlockSpec((tm, d), lambda i: (i, 0)),
                  pl.BlockSpec((1, d), lambda i: (0, 0)),
                  pl.BlockSpec((qd, d), lambda i: (0, 0), pipeline_mode=pl.Buffered(1))],
        out_specs=(pl.BlockSpec((d, tm), lambda i: (0, i)),
                   pl.BlockSpec((qd, tm), lambda i: (0, i))),
        compiler_params=_params(("parallel",), vmem),
        name="norm_query",
    )(x, g.reshape(1, d), wq_t)


def _compare_exchange(v, i, j):
    hi, lo = jnp.maximum(v[i], v[j]), jnp.minimum(v[i], v[j])
    v[i], v[j] = hi, lo


def _bitonic_merge(v):
    d = TOPK // 2
    while d >= 1:
        for i in range(TOPK):
            if not i & d:
                _compare_exchange(v, i, i + d)
        d //= 2


def _merge_top(top, other):
    out = list(top)
    for i, x in enumerate(other):
        out[TOPK - 1 - i] = jnp.maximum(top[TOPK - 1 - i], x)
    _bitonic_merge(out)
    return out


def _top_values(s):
    nk = s.shape[0]
    per_sublane = [s[SUBLANES * i:SUBLANES * (i + 1), :] for i in range(nk // SUBLANES)]
    top = None
    for base in range(0, len(per_sublane), TOPK):
        v = per_sublane[base:base + TOPK]
        for i, j in SORT16:
            _compare_exchange(v, i, j)
        top = v if top is None else _merge_top(top, v)
    shift = SUBLANES // 2
    while shift >= 1:
        top = _merge_top(top, [pltpu.roll(x, shift, 0) for x in top])
        shift //= 2
    return top


def _pair_top_values(a, b):
    split = 4
    top = [a[0] + b[j] for j in range(TOPK)]
    for i in range(1, split):
        top = _merge_top(top, [a[i] + b[j] for j in range(TOPK // (i + 1))])
    for j in range(TOPK // (split + 1)):
        top = _merge_top(top, [a[i] + b[j] for i in range(split, TOPK) if (i + 1) * (j + 1) <= TOPK])
    return top


def _topk_kernel(q_ref, k_ref, s1_ref, s2_ref, thr_ref, cst_ref, *, heads, nk, hk):
    sublane = lax.broadcasted_iota(jnp.int32, thr_ref.shape, 0)
    thr = jnp.zeros(thr_ref.shape, F32)
    cst = jnp.zeros(cst_ref.shape, F32)
    for h in range(heads):
        vals = []
        for p, s_ref in enumerate((s1_ref, s2_ref)):
            q = q_ref[(2 * h + p) * hk:(2 * h + p + 1) * hk, :]
            s = jnp.dot(k_ref[h, p], q, preferred_element_type=F32)
            s_ref[h * nk:(h + 1) * nk, :] = s
            vals.append(_top_values(s))
        top = _pair_top_values(vals[0], vals[1])
        z = jnp.exp(top[0] - top[0])
        for k in range(1, TOPK):
            z = z + jnp.exp(top[k] - top[0])
        thr = jnp.where(sublane == h, top[TOPK - 1], thr)
        cst = jnp.where(sublane == h, top[0] + jnp.log(z), cst)
    thr_ref[...] = thr
    cst_ref[...] = cst


def peer_topk(q_t, keys, tn=256):
    qd, n = q_t.shape
    heads, _, nk, hk = keys.shape
    assert nk % (SUBLANES * TOPK) == 0 and heads == SUBLANES
    tn = _tile(n, tn)
    vmem = 2 * (qd * tn * 2 + keys.size * 2 + 2 * heads * nk * tn * 4 + 2 * SUBLANES * tn * 4)
    return pl.pallas_call(
        functools.partial(_topk_kernel, heads=heads, nk=nk, hk=hk),
        out_shape=(jax.ShapeDtypeStruct((heads * nk, n), F32),
                   jax.ShapeDtypeStruct((heads * nk, n), F32),
                   jax.ShapeDtypeStruct((heads, n), F32),
                   jax.ShapeDtypeStruct((heads, n), F32)),
        grid=(n // tn,),
        in_specs=[pl.BlockSpec((qd, tn), lambda i: (0, i)),
                  pl.BlockSpec((heads, 2, nk, hk), lambda i: (0, 0, 0, 0))],
        out_specs=(pl.BlockSpec((heads * nk, tn), lambda i: (0, i)),
                   pl.BlockSpec((heads * nk, tn), lambda i: (0, i)),
                   pl.BlockSpec((heads, tn), lambda i: (0, i)),
                   pl.BlockSpec((heads, tn), lambda i: (0, i))),
        compiler_params=_params(("parallel",), vmem),
        name="peer_topk",
    )(q_t, keys)


def _peer_kernel(xt_ref, s1_ref, s2_ref, thr_ref, cst_ref, u_ref, vt_ref, o_ref,
                 h_ref, w_ref, *, heads, nk, lane_chunk):
    j = pl.program_id(1)
    te, tm = w_ref.shape
    per = te // nk
    prev_tile = jnp.maximum(j - 1, 0)

    @pl.when(j == 0)
    def _():
        o_ref[...] = jnp.zeros_like(o_ref)
        h_ref[...] = jnp.zeros_like(h_ref)

    for c in range(tm // lane_chunk):
        ls = slice(c * lane_chunk, (c + 1) * lane_chunk)
        for ii in range(per):
            row1 = prev_tile * per + ii
            gate = None
            for h in range(heads):
                s = s2_ref[h * nk:(h + 1) * nk, ls] + s1_ref[pl.ds(h * nk + row1, 1), ls]
                keep = s >= thr_ref[h:h + 1, ls]
                term = jnp.where(keep, jnp.exp(s - cst_ref[h:h + 1, ls]), 0.0)
                gate = term if gate is None else gate + term
            hh = h_ref[ii * nk:(ii + 1) * nk, ls]
            gelu = 0.5 * hh * (1.0 + lax.erf(hh * SQRT_HALF))
            w_ref[ii * nk:(ii + 1) * nk, ls] = (gate * gelu).astype(w_ref.dtype)
        h_ref[:, ls] = jnp.dot(u_ref[...], xt_ref[:, ls], preferred_element_type=F32)
        o_ref[:, ls] += jnp.dot(vt_ref[...], w_ref[:, ls], preferred_element_type=F32)


def peer_dense(xn_t, s1_t, s2_t, thr, cst, u, v_t, tm=1024, te=512):
    d, n = xn_t.shape
    ne = u.shape[0]
    heads = thr.shape[0]
    nk = s1_t.shape[0] // heads
    assert ne == nk * nk
    tm, te = _tile(n, tm), max(_tile(ne, te), nk)
    nt = ne // te
    once = pl.Buffered(1)
    vmem = (d * tm * 2 + 2 * heads * nk * tm * 4 + 2 * SUBLANES * tm * 4 + d * tm * 4
            + 2 * 2 * te * d * 2 + te * tm * 6)
    return pl.pallas_call(
        functools.partial(_peer_kernel, heads=heads, nk=nk, lane_chunk=_tile(tm, 2 * LANES)),
        out_shape=jax.ShapeDtypeStruct((d, n), F32),
        grid=(n // tm, nt + 1),
        in_specs=[pl.BlockSpec((d, tm), lambda i, j: (0, i), pipeline_mode=once),
                  pl.BlockSpec((heads * nk, tm), lambda i, j: (0, i), pipeline_mode=once),
                  pl.BlockSpec((heads * nk, tm), lambda i, j: (0, i), pipeline_mode=once),
                  pl.BlockSpec((heads, tm), lambda i, j: (0, i), pipeline_mode=once),
                  pl.BlockSpec((heads, tm), lambda i, j: (0, i), pipeline_mode=once),
                  pl.BlockSpec((te, d), lambda i, j: (jnp.minimum(j, nt - 1), 0)),
                  pl.BlockSpec((d, te), lambda i, j: (0, jnp.maximum(j - 1, 0)))],
        out_specs=pl.BlockSpec((d, tm), lambda i, j: (0, i), pipeline_mode=once),
        scratch_shapes=[pltpu.VMEM((te, tm), F32), pltpu.VMEM((te, tm), BF16)],
        compiler_params=_params(("parallel", "arbitrary"), vmem),
        name="peer_dense",
    )(xn_t, s1_t, s2_t, thr, cst, u, v_t)


def _final_kernel(x_ref, pt_ref, g_ref, o_ref):
    x = x_ref[...] + pt_ref[...].T
    r = lax.rsqrt(jnp.mean(x * x, axis=-1, keepdims=True) + EPS)
    o_ref[...] = (x * r) * g_ref[...]


def residual_rmsnorm_t(x, p_t, g, tm=256):
    n, d = x.shape
    tm = _tile(n, tm)
    return pl.pallas_call(
        _final_kernel,
        out_shape=jax.ShapeDtypeStruct((n, d), F32),
        grid=(n // tm,),
        in_specs=[pl.BlockSpec((tm, d), lambda i: (i, 0)),
                  pl.BlockSpec((d, tm), lambda i: (0, i)),
                  pl.BlockSpec((1, d), lambda i: (0, 0))],
        out_specs=pl.BlockSpec((tm, d), lambda i: (i, 0)),
        compiler_params=_params(("parallel",), 2 * 3 * tm * d * 4 + tm * d * 4),
        name="residual_rmsnorm_t",
    )(x, p_t, g.reshape(1, d))


def _residual_t_kernel(x_ref, pt_ref, o_ref):
    o_ref[...] = x_ref[...] + pt_ref[...].T


def residual_t(x, p_t, tm=256):
    n, d = x.shape
    tm = _tile(n, tm)
    return pl.pallas_call(
        _residual_t_kernel,
        out_shape=jax.ShapeDtypeStruct((n, d), F32),
        grid=(n // tm,),
        in_specs=[pl.BlockSpec((tm, d), lambda i: (i, 0)),
                  pl.BlockSpec((d, tm), lambda i: (0, i))],
        out_specs=pl.BlockSpec((tm, d), lambda i: (i, 0)),
        compiler_params=_params(("parallel",), 2 * 3 * tm * d * 4 + tm * d * 4),
        name="residual_t",
    )(x, p_t)


def _layer(x, hist, lw, chunk):
    b, t, d = x.shape
    n = b * t
    d_a = lw["conv_w"].shape[1]
    d_b = lw["ln_b_g"].shape[0]
    x2 = x.reshape(n, d)
    xn = rmsnorm_cast(x2, lw["norm_mix_g"])
    a = glu_proj(xn, lw["w_in"], d_a)
    uv = proj(xn, lw["w_in"], 2 * d_a, 2 * d_b)
    gates = proj(xn, lw["w_in"], 2 * d_a + 2 * d_b, 2 * d, bias=lw["b_gate"])
    act = conv_ln(a.reshape(b, t, d_a), hist, lw["conv_w"], lw["conv_b"], lw["ln_a_g"], lw["ln_a_b"])
    p, v = gmlp_mix(uv, lw["ln_b_g"], lw["ln_b_b"], lw["w_spatial"], lw["b_spatial"], chunk)
    m = merge_proj(act.reshape(n, d_a), p, gates, lw["w_a_out"], lw["w_b_out"])
    x1 = residual_proj(m, lw["w_o"], x2)
    xn_t, q_t = norm_query(x1, lw["norm_ffn_g"], lw["w_query_t"])
    s1_t, s2_t, thr, cst = peer_topk(q_t, lw["sub_keys"])
    peer_t = peer_dense(xn_t, s1_t, s2_t, thr, cst, lw["expert_u"], lw["expert_v_t"])
    return x1, peer_t, a.reshape(b, t, d_a), v.reshape(b, t, d_b)


def kernel(x_prompt, x_sample, cache_conv, norm_mix_g, w_in, b_gate, conv_w, conv_b, ln_a_g, ln_a_b, w_a_out, ln_b_g, ln_b_b, w_spatial, b_spatial, w_b_out, w_o, norm_ffn_g, w_query, sub_keys, expert_u, expert_v, norm_final_g):
    depth = w_in.shape[0]
    kw = conv_w.shape[1]
    d_a = conv_w.shape[2]
    gmlp_chunk = w_spatial.shape[2]
    xp, xs = x_prompt, x_sample
    bp, tp, d = xp.shape
    bs, ts, _ = xs.shape
    assert tp % gmlp_chunk == 0 and ts <= gmlp_chunk and ts % STREAM_CHUNK == 0
    assert tp >= kw - 1 and ts >= kw - 1
    conv_p, conv_s, v_s = [], [], []
    for l in range(depth):
        lw = dict(
            norm_mix_g=norm_mix_g[l], w_in=w_in[l].astype(BF16), b_gate=b_gate[l],
            conv_w=conv_w[l], conv_b=conv_b[l], ln_a_g=ln_a_g[l], ln_a_b=ln_a_b[l],
            w_a_out=w_a_out[l].astype(BF16), ln_b_g=ln_b_g[l], ln_b_b=ln_b_b[l],
            w_spatial=w_spatial[l], b_spatial=b_spatial[l], w_b_out=w_b_out[l].astype(BF16),
            w_o=w_o[l].astype(BF16), norm_ffn_g=norm_ffn_g[l],
            w_query_t=w_query[l].T.astype(BF16), sub_keys=sub_keys[l].astype(BF16),
            expert_u=expert_u[l].astype(BF16), expert_v_t=expert_v[l].T.astype(BF16))
        last = l == depth - 1
        hist_p = jnp.zeros((bp, HALO_ROWS, d_a), F32)
        hist_s = jnp.pad(cache_conv[l].astype(F32), ((0, 0), (HALO_ROWS - (kw - 1), 0), (0, 0)))
        outs = []
        for x, hist, chunk in ((xp, hist_p, gmlp_chunk), (xs, hist_s, ts)):
            b, t, _ = x.shape
            x1, peer_t, a, v = _layer(x, hist, lw, chunk)
            if last:
                y = residual_rmsnorm_t(x1, peer_t, norm_final_g)
            else:
                y = residual_t(x1, peer_t)
            outs.append((y.reshape(b, t, d), a, v))
        (xp, a_p, _), (xs, a_s, v_new) = outs
        conv_p.append(a_p[:, tp - (kw - 1):])
        conv_s.append(a_s[:, ts - (kw - 1):])
        v_s.append(v_new)
    return (xp, xs, jnp.stack(conv_p, axis=0), jnp.stack(conv_s, axis=0), jnp.stack(v_s, axis=0))
```

```python
import functools
import math

import jax
import jax.numpy as jnp
from jax import lax
from jax.experimental import pallas as pl
from jax.experimental.pallas import tpu as pltpu

F32 = jnp.float32
BF16 = jnp.bfloat16

EPS = 1e-6
STREAM_CHUNK = 64
TOPK = 16
SUBLANES = 8
LANES = 128
HALO_ROWS = 32
VMEM_LIMIT_CAP = 60000 * 1024
SQRT_HALF = 1.0 / math.sqrt(2.0)


def _tile(n, pref):
    t = min(n, pref)
    while n % t:
        t //= 2
    return t


def _params(semantics, vmem_bytes):
    limit = int(min(VMEM_LIMIT_CAP, max(vmem_bytes * 5 // 4 + (4 << 20), 16 << 20)))
    return pltpu.CompilerParams(dimension_semantics=semantics, vmem_limit_bytes=limit)


def _sorting_network(n):
    pairs = []
    p = 1
    while p < n:
        k = p
        while k >= 1:
            for j in range(k % p, n - k, 2 * k):
                for i in range(min(k, n - j - k)):
                    if (i + j) // (2 * p) == (i + j + k) // (2 * p):
                        pairs.append((i + j, i + j + k))
            k //= 2
        p *= 2
    return pairs


SORT16 = _sorting_network(TOPK)


def _rmsnorm_kernel(x_ref, g_ref, o_ref):
    x = x_ref[...]
    r = lax.rsqrt(jnp.mean(x * x, axis=-1, keepdims=True) + EPS)
    o_ref[...] = ((x * r) * g_ref[...]).astype(o_ref.dtype)


def rmsnorm_cast(x, g, tm=512):
    n, d = x.shape
    tm = _tile(n, tm)
    return pl.pallas_call(
        _rmsnorm_kernel,
        out_shape=jax.ShapeDtypeStruct((n, d), BF16),
        grid=(n // tm,),
        in_specs=[pl.BlockSpec((tm, d), lambda i: (i, 0)),
                  pl.BlockSpec((1, d), lambda i: (0, 0))],
        out_specs=pl.BlockSpec((tm, d), lambda i: (i, 0)),
        compiler_params=_params(("parallel",), 2 * tm * d * 6),
        name="rmsnorm_cast",
    )(x, g.reshape(1, d))


def _glu_kernel(x_ref, wl_ref, wg_ref, o_ref):
    x = x_ref[...]
    lin = jnp.dot(x, wl_ref[...], preferred_element_type=F32)
    gate = jnp.dot(x, wg_ref[...], preferred_element_type=F32)
    o_ref[...] = lin * jax.nn.sigmoid(gate)


def glu_proj(xn, w, d_a, tm=1024, tn=512):
    n, d = xn.shape
    tm, tn = _tile(n, tm), _tile(d_a, tn)
    nb = d_a // tn
    vmem = 2 * (tm * d * 2 + 2 * d * tn * 2 + tm * tn * 4)
    return pl.pallas_call(
        _glu_kernel,
        out_shape=jax.ShapeDtypeStruct((n, d_a), F32),
        grid=(n // tm, nb),
        in_specs=[pl.BlockSpec((tm, d), lambda i, j: (i, 0)),
                  pl.BlockSpec((d, tn), lambda i, j: (0, j)),
                  pl.BlockSpec((d, tn), lambda i, j: (0, j + nb))],
        out_specs=pl.BlockSpec((tm, tn), lambda i, j: (i, j)),
        compiler_params=_params(("parallel", "arbitrary"), vmem),
        name="glu_proj",
    )(xn, w, w)


def _proj_kernel(x_ref, w_ref, o_ref):
    o_ref[...] = jnp.dot(x_ref[...], w_ref[...], preferred_element_type=F32)


def _proj_gate_kernel(x_ref, w_ref, b_ref, o_ref):
    h = jnp.dot(x_ref[...], w_ref[...], preferred_element_type=F32)
    o_ref[...] = jax.nn.sigmoid(h + b_ref[...])


def proj(xn, w, col0, ncols, bias=None, tm=1024, tn=1024):
    n, d = xn.shape
    tm, tn = _tile(n, tm), _tile(math.gcd(ncols, col0) if col0 else ncols, tn)
    off = col0 // tn
    vmem = 2 * (tm * d * 2 + d * tn * 2 + tm * tn * 4)
    in_specs = [pl.BlockSpec((tm, d), lambda i, j: (i, 0)),
                pl.BlockSpec((d, tn), lambda i, j: (0, j + off))]
    args = [xn, w]
    kern = _proj_kernel
    if bias is not None:
        in_specs.append(pl.BlockSpec((1, tn), lambda i, j: (0, j)))
        args.append(bias.reshape(1, ncols))
        kern = _proj_gate_kernel
    return pl.pallas_call(
        kern,
        out_shape=jax.ShapeDtypeStruct((n, ncols), F32),
        grid=(n // tm, ncols // tn),
        in_specs=in_specs,
        out_specs=pl.BlockSpec((tm, tn), lambda i, j: (i, j)),
        compiler_params=_params(("parallel", "arbitrary"), vmem),
        name="proj_gate" if bias is not None else "proj",
    )(*args)


def _layernorm(x, g, b):
    mu = jnp.mean(x, axis=-1, keepdims=True)
    xc = x - mu
    var = jnp.mean(xc * xc, axis=-1, keepdims=True)
    return (xc * lax.rsqrt(var + EPS)) * g + b


def _conv_act_rows(window, cw_ref, cb_ref, g_ref, b_ref, cbuf_ref, *, kw, cols):
    rows, d_a = cbuf_ref.shape
    lead = HALO_ROWS - (kw - 1)
    span = rows + HALO_ROWS
    for c in range(d_a // cols):
        cs = slice(c * cols, (c + 1) * cols)
        win = window(cs)
        acc = jnp.broadcast_to(cb_ref[:, cs], (rows, cols))
        for s in range(SUBLANES):
            shifted = win if s == 0 else pltpu.roll(win, span - s, 0)
            for off in range(s, HALO_ROWS + 1, SUBLANES):
                k = off - lead
                if 0 <= k < kw:
                    acc = acc + shifted[off - s:off - s + rows, :] * cw_ref[k:k + 1, cs]
        cbuf_ref[:, cs] = acc
    y =_layernorm(cbuf_ref[...], g_ref[...], b_ref[...])
    return y * jax.nn.sigmoid(y)


def _conv_kernel(a_ref, prev_ref, hist_ref, cw_ref, cb_ref, g_ref, b_ref, o_ref,
                 buf_ref, cbuf_ref, *, tt, kw, cols):
    t = pl.program_id(1)
    buf_ref[0:HALO_ROWS, :] = jnp.where(t == 0, hist_ref[0], prev_ref[0])
    buf_ref[HALO_ROWS:HALO_ROWS + tt, :] = a_ref[0]
    rows = cbuf_ref.shape[0]

    def row_chunk(rc, carry):
        r0 = pl.multiple_of(rc * rows, rows)
        act = _conv_act_rows(lambda cs: buf_ref[pl.ds(r0, rows + HALO_ROWS), cs],
                             cw_ref, cb_ref, g_ref, b_ref, cbuf_ref, kw=kw, cols=cols)
        o_ref[0, pl.ds(r0, rows), :] = act.astype(o_ref.dtype)
        return carry

    lax.fori_loop(0, tt // rows, row_chunk, 0)


def conv_ln(a, hist, conv_w, conv_b, ln_g, ln_b, tt=512):
    b, t, d_a = a.shape
    kw = conv_w.shape[0]
    assert kw - 1 <= HALO_ROWS and t % HALO_ROWS == 0
    tt = _tile(t, tt)
    per = tt // HALO_ROWS
    rows = _tile(tt, 64)
    vmem = 2 * (tt * d_a * 4 + 2 * HALO_ROWS * d_a * 4 + tt * d_a * 2) + (tt + HALO_ROWS + rows) * d_a * 4
    vec = lambda v: v.reshape(1, d_a)
    return pl.pallas_call(
        functools.partial(_conv_kernel, tt=tt, kw=kw, cols=LANES),
        out_shape=jax.ShapeDtypeStruct((b, t, d_a), BF16),
        grid=(b, t // tt),
        in_specs=[pl.BlockSpec((1, tt, d_a), lambda i, j: (i, j, 0)),
                  pl.BlockSpec((1, HALO_ROWS, d_a), lambda i, j: (i, jnp.maximum(j * per - 1, 0), 0)),
                  pl.BlockSpec((1, HALO_ROWS, d_a), lambda i, j: (i, 0, 0)),
                  pl.BlockSpec((kw, d_a), lambda i, j: (0, 0)),
                  pl.BlockSpec((1, d_a), lambda i, j: (0, 0)),
                  pl.BlockSpec((1, d_a), lambda i, j: (0, 0)),
                  pl.BlockSpec((1, d_a), lambda i, j: (0, 0))],
        out_specs=pl.BlockSpec((1, tt, d_a), lambda i, j: (i, j, 0)),
        scratch_shapes=[pltpu.VMEM((HALO_ROWS + tt, d_a), F32),
                        pltpu.VMEM((rows, d_a), F32)],
        compiler_params=_params(("parallel", "arbitrary"), vmem),
        name="conv_ln",
    )(a, a, hist, conv_w, vec(conv_b), vec(ln_g), vec(ln_b))


def _gmlp_kernel(uv_ref, g_ref, b_ref, ws_ref, bs_ref, p_ref, *v_refs, chunk, d_b):
    groups = ws_ref.shape[0]
    gc = d_b // groups
    tg = uv_ref.shape[0]
    v = _layernorm(uv_ref[:, d_b:], g_ref[...], b_ref[...])
    for r in v_refs:
        r[...] = v
    v_ref = v_refs[-1]
    row = lax.broadcasted_iota(jnp.int32, (chunk, chunk), 0) // STREAM_CHUNK
    col = lax.broadcasted_iota(jnp.int32, (chunk, chunk), 1) // STREAM_CHUNK
    causal = row >= col
    for g in range(groups):
        ws = jnp.where(causal, ws_ref[g], 0.0).astype(BF16)
        bias = bs_ref[g]
        for c in range(tg // chunk):
            rs = slice(c * chunk, (c + 1) * chunk)
            cs = slice(g * gc, (g + 1) * gc)
            z = jnp.dot(ws, v_ref[rs, cs].astype(BF16), preferred_element_type=F32) + bias
            p_ref[rs, cs] = (uv_ref[rs, cs] * z).astype(p_ref.dtype)


def gmlp_mix(uv, ln_g, ln_b, w_spatial, b_spatial, chunk, emit_v, tg=512):
    n, two_db = uv.shape
    d_b = two_db // 2
    groups = w_spatial.shape[0]
    ws = w_spatial[:, :chunk, :chunk]
    bs = b_spatial[:, :chunk].reshape(groups, chunk, 1)
    tg = max(_tile(n, tg), chunk)
    assert n % tg == 0 and tg % chunk == 0
    vmem = 2 * (tg * two_db * 4 + tg * d_b * 6) + 2 * groups * chunk * (chunk + LANES) * 4 + tg * d_b * 4
    row_block = pl.BlockSpec((tg, d_b), lambda i: (i, 0))
    return pl.pallas_call(
        functools.partial(_gmlp_kernel, chunk=chunk, d_b=d_b),
        out_shape=(jax.ShapeDtypeStruct((n, d_b), BF16),) + ((jax.ShapeDtypeStruct((n, d_b), F32),) if emit_v else ()),
        grid=(n // tg,),
        in_specs=[pl.BlockSpec((tg, two_db), lambda i: (i, 0)),
                  pl.BlockSpec((1, d_b), lambda i: (0, 0)),
                  pl.BlockSpec((1, d_b), lambda i: (0, 0)),
                  pl.BlockSpec((groups, chunk, chunk), lambda i: (0, 0, 0)),
                  pl.BlockSpec((groups, chunk, 1), lambda i: (0, 0, 0))],
        out_specs=(row_block,) + ((row_block,) if emit_v else ()),
        scratch_shapes=[pltpu.VMEM((tg, d_b), F32)],
        compiler_params=_params(("parallel",), vmem),
        name="gmlp_mix",
    )(uv, ln_g.reshape(1, d_b), ln_b.reshape(1, d_b), ws, bs)


def _merge_kernel(act_ref, p_ref, wa_ref, wb_ref, ga_ref, gb_ref, o_ref):
    ya = jnp.dot(act_ref[...], wa_ref[...], preferred_element_type=F32)
    yb = jnp.dot(p_ref[...], wb_ref[...], preferred_element_type=F32)
    o_ref[...] = (ga_ref[...] * ya + gb_ref[...] * yb).astype(o_ref.dtype)


def merge_proj(act, p, gates, wa, wb, tm=1024, tn=512):
    n, d_a = act.shape
    d_b = p.shape[1]
    d = wa.shape[1]
    tm, tn = _tile(n, tm), _tile(d, tn)
    nb = d // tn
    vmem = 2 * (tm * (d_a + d_b) * 2 + (d_a + d_b) * tn * 2 + 2 * tm * tn * 4 + tm * tn * 2)
    return pl.pallas_call(
        _merge_kernel,
        out_shape=jax.ShapeDtypeStruct((n, d), BF16),
        grid=(n // tm, nb),
        in_specs=[pl.BlockSpec((tm, d_a), lambda i, j: (i, 0)),
                  pl.BlockSpec((tm, d_b), lambda i, j: (i, 0)),
                  pl.BlockSpec((d_a, tn), lambda i, j: (0, j)),
                  pl.BlockSpec((d_b, tn), lambda i, j: (0, j)),
                  pl.BlockSpec((tm, tn), lambda i, j: (i, j)),
                  pl.BlockSpec((tm, tn), lambda i, j: (i, j + nb))],
        out_specs=pl.BlockSpec((tm, tn), lambda i, j: (i, j)),
        compiler_params=_params(("parallel", "arbitrary"), vmem),
        name="merge_proj",
    )(act, p, wa, wb, gates, gates)


def _residual_proj_kernel(m_ref, w_ref, x_ref, o_ref):
    o_ref[...] = x_ref[...] + jnp.dot(m_ref[...], w_ref[...], preferred_element_type=F32)


def residual_proj(m, w, x, tm=1024, tn=1024):
    n, k = m.shape
    d = w.shape[1]
    tm, tn = _tile(n, tm), _tile(d, tn)
    vmem = 2 * (tm * k * 2 + k * tn * 2 + 2 * tm * tn * 4)
    return pl.pallas_call(
        _residual_proj_kernel,
        out_shape=jax.ShapeDtypeStruct((n, d), F32),
        grid=(n // tm, d // tn),
        in_specs=[pl.BlockSpec((tm, k), lambda i, j: (i, 0)),
                  pl.BlockSpec((k, tn), lambda i, j: (0, j)),
                  pl.BlockSpec((tm, tn), lambda i, j: (i, j))],
        out_specs=pl.BlockSpec((tm, tn), lambda i, j: (i, j)),
        compiler_params=_params(("parallel", "arbitrary"), vmem),
        name="residual_proj",
    )(m, w, x)


def _norm_query_kernel(x_ref, g_ref, wq_ref, xt_ref, qt_ref):
    x = x_ref[...]
    r = lax.rsqrt(jnp.mean(x * x, axis=-1, keepdims=True) + EPS)
    xt_ref[...] = ((x * r) * g_ref[...]).T.astype(xt_ref.dtype)
    qt_ref[...] = jnp.dot(wq_ref[...], xt_ref[...], preferred_element_type=F32).astype(qt_ref.dtype)


def norm_query(x, g, wq_t, tm=512):
    n, d = x.shape
    qd = wq_t.shape[0]
    tm = _tile(n, tm)
    vmem = 2 * (tm * d * 4 + d * tm * 2 + qd * tm * 2) + qd * d * 2 + tm * d * 8
    return pl.pallas_call(
        _norm_query_kernel,
        out_shape=(jax.ShapeDtypeStruct((d, n), BF16), jax.ShapeDtypeStruct((qd, n), BF16)),
        grid=(n // tm,),
        in_specs=[pl.BlockSpec((tm, d), lambda i: (i, 0)),
                  pl.BlockSpec((1, d), lambda i: (0, 0)),
                  pl.BlockSpec((qd, d), lambda i: (0, 0), pipeline_mode=pl.Buffered(1))],
        out_specs=(pl.BlockSpec((d, tm), lambda i: (0, i)),
                   pl.BlockSpec((qd, tm), lambda i: (0, i))),
        compiler_params=_params(("parallel",), vmem),
        name="norm_query",
    )(x, g.reshape(1, d), wq_t)


def _compare_exchange(v, i, j):
    hi, lo = jnp.maximum(v[i], v[j]), jnp.minimum(v[i], v[j])
    v[i], v[j] = hi, lo


def _bitonic_merge(v):
    d = TOPK // 2
    while d >= 1:
        for i in range(TOPK):
            if not i & d:
                _compare_exchange(v, i, i + d)
        d //= 2


def _merge_top(top, other):
    out = list(top)
    for i, x in enumerate(other):
        out[TOPK - 1 - i] = jnp.maximum(top[TOPK - 1 - i], x)
    _bitonic_merge(out)
    return out


def _top_values(s):
    nk = s.shape[0]
    per_sublane = [s[SUBLANES * i:SUBLANES * (i + 1), :] for i in range(nk // SUBLANES)]
    top = None
    for base in range(0, len(per_sublane), TOPK):
        v = per_sublane[base:base + TOPK]
        for i, j in SORT16:
            _compare_exchange(v, i, j)
        top = v if top is None else _merge_top(top, v)
    shift = SUBLANES // 2
    while shift >= 1:
        top = _merge_top(top, [pltpu.roll(x, shift, 0) for x in top])
        shift //= 2
    return top


def _pair_top_values(a, b):
    split = 4
    top = [a[0] + b[j] for j in range(TOPK)]
    for i in range(1, split):
        top = _merge_top(top, [a[i] + b[j] for j in range(TOPK // (i + 1))])
    for j in range(TOPK // (split + 1)):
        top = _merge_top(top, [a[i] + b[j] for i in range(split, TOPK) if (i + 1) * (j + 1) <= TOPK])
    return top


def _topk_kernel(q_ref, k_ref, s1_ref, s2_ref, thr_ref, cst_ref, *, heads, nk, hk):
    sublane = lax.broadcasted_iota(jnp.int32, thr_ref.shape, 0)
    packed = [None, None]
    for h in range(heads):
        for p, s_ref in enumerate((s1_ref, s2_ref)):
            q = q_ref[(2 * h + p) * hk:(2 * h + p + 1) * hk, :]
            s = jnp.dot(k_ref[h, p], q, preferred_element_type=F32)
            s_ref[h * nk:(h + 1) * nk, :] = s
            vals = _top_values(s)
            packed[p] = vals if h == 0 else [jnp.where(sublane == h, v, acc) for v, acc in zip(vals, packed[p])]
    top = _pair_top_values(packed[0], packed[1])
    z = jnp.exp(top[0] - top[0])
    for k in range(1, TOPK):
        z = z + jnp.exp(top[k] - top[0])
    thr_ref[...] = top[TOPK - 1]
    cst_ref[...] = top[0] + jnp.log(z)


def peer_topk(q_t, keys, tn=256):
    qd, n = q_t.shape
    heads, _, nk, hk = keys.shape
    assert nk % (SUBLANES * TOPK) == 0 and heads == SUBLANES
    tn = _tile(n, tn)
    vmem = 2 * (qd * tn * 2 + keys.size * 2 + 2 * heads * nk * tn * 4 + 2 * SUBLANES * tn * 4)
    return pl.pallas_call(
        functools.partial(_topk_kernel, heads=heads, nk=nk, hk=hk),
        out_shape=(jax.ShapeDtypeStruct((heads * nk, n), F32),
                   jax.ShapeDtypeStruct((heads * nk, n), F32),
                   jax.ShapeDtypeStruct((heads, n), F32),
                   jax.ShapeDtypeStruct((heads, n), F32)),
        grid=(n // tn,),
        in_specs=[pl.BlockSpec((qd, tn), lambda i: (0, i)),
                  pl.BlockSpec((heads, 2, nk, hk), lambda i: (0, 0, 0, 0))],
        out_specs=(pl.BlockSpec((heads * nk, tn), lambda i: (0, i)),
                   pl.BlockSpec((heads * nk, tn), lambda i: (0, i)),
                   pl.BlockSpec((heads, tn), lambda i: (0, i)),
                   pl.BlockSpec((heads, tn), lambda i: (0, i))),
        compiler_params=_params(("parallel",), vmem),
        name="peer_topk",
    )(q_t, keys)


def _peer_kernel(xt_ref, s1_ref, s2_ref, thr_ref, cst_ref, u_ref, vt_ref, o_ref,
                 h_ref, w_ref, *, heads, nk, lane_chunk):
    j = pl.program_id(1)
    last = pl.num_programs(1) - 1
    te, tm = w_ref.shape
    per = te // nk
    chunks = [slice(c * lane_chunk, (c + 1) * lane_chunk) for c in range(tm // lane_chunk)]

    def weights(ls):
        for ii in range(per):
            row1 = (j - 1) * per + ii
            gate = None
            for h in range(heads):
                s = s2_ref[h * nk:(h + 1) * nk, ls] + s1_ref[pl.ds(h * nk + row1, 1), ls]
                keep = s >= thr_ref[h:h + 1, ls]
                term = jnp.where(keep, jnp.exp(s - cst_ref[h:h + 1, ls]), 0.0)
                gate = term if gate is None else gate + term
            hh = h_ref[ii * nk:(ii + 1) * nk, ls]
            gelu = 0.5 * hh * (1.0 + lax.erf(hh * SQRT_HALF))
            w_ref[ii * nk:(ii + 1) * nk, ls] = (gate * gelu).astype(w_ref.dtype)

    def hidden(ls):
        h_ref[:, ls] = jnp.dot(u_ref[...], xt_ref[:, ls], preferred_element_type=F32)

    def accumulate(ls):
        o_ref[:, ls] += jnp.dot(vt_ref[...], w_ref[:, ls], preferred_element_type=F32)

    @pl.when(j == 0)
    def _():
        o_ref[...] = jnp.zeros_like(o_ref)
        for ls in chunks:
            hidden(ls)

    @pl.when(jnp.logical_and(j > 0, j < last))
    def _():
        for ls in chunks:
            weights(ls)
            hidden(ls)
            accumulate(ls)

    @pl.when(j == last)
    def _():
        for ls in chunks:
            weights(ls)
            accumulate(ls)


def peer_dense(xn_t, s1_t, s2_t, thr, cst, u, v_t, tm=1024, te=512):
    d, n = xn_t.shape
    ne = u.shape[0]
    heads = thr.shape[0]
    nk = s1_t.shape[0] // heads
    assert ne == nk * nk
    tm, te = _tile(n, tm), max(_tile(ne, te), nk)
    nt = ne // te
    once = pl.Buffered(1)
    vmem = (d * tm * 2 + 2 * heads * nk * tm * 4 + 2 * SUBLANES * tm * 4 + d * tm * 4
            + 2 * 2 * te * d * 2 + te * tm * 6)
    return pl.pallas_call(
        functools.partial(_peer_kernel, heads=heads, nk=nk, lane_chunk=_tile(tm, 2 * LANES)),
        out_shape=jax.ShapeDtypeStruct((d, n), F32),
        grid=(n // tm, nt + 1),
        in_specs=[pl.BlockSpec((d, tm), lambda i, j: (0, i), pipeline_mode=once),
                  pl.BlockSpec((heads * nk, tm), lambda i, j: (0, i), pipeline_mode=once),
                  pl.BlockSpec((heads * nk, tm), lambda i, j: (0, i), pipeline_mode=once),
                  pl.BlockSpec((heads, tm), lambda i, j: (0, i), pipeline_mode=once),
                  pl.BlockSpec((heads, tm), lambda i, j: (0, i), pipeline_mode=once),
                  pl.BlockSpec((te, d), lambda i, j: (jnp.minimum(j, nt - 1), 0)),
                  pl.BlockSpec((d, te), lambda i, j: (0, jnp.maximum(j - 1, 0)))],
        out_specs=pl.BlockSpec((d, tm), lambda i, j: (0, i), pipeline_mode=once),
        scratch_shapes=[pltpu.VMEM((te, tm), F32), pltpu.VMEM((te, tm), BF16)],
        compiler_params=_params(("parallel", "arbitrary"), vmem),
        name="peer_dense",
    )(xn_t, s1_t, s2_t, thr, cst, u, v_t)


def _final_kernel(x_ref, pt_ref, g_ref, o_ref):
    x = x_ref[...] + pt_ref[...].T
    r = lax.rsqrt(jnp.mean(x * x, axis=-1, keepdims=True) + EPS)
    o_ref[...] = (x * r) * g_ref[...]


def residual_rmsnorm_t(x, p_t, g, tm=256):
    n, d = x.shape
    tm = _tile(n, tm)
    return pl.pallas_call(
        _final_kernel,
        out_shape=jax.ShapeDtypeStruct((n, d), F32),
        grid=(n // tm,),
        in_specs=[pl.BlockSpec((tm, d), lambda i: (i, 0)),
                  pl.BlockSpec((d, tm), lambda i: (0, i)),
                  pl.BlockSpec((1, d), lambda i: (0, 0))],
        out_specs=pl.BlockSpec((tm, d), lambda i: (i, 0)),
        compiler_params=_params(("parallel",), 2 * 3 * tm * d * 4 + tm * d * 4),
        name="residual_rmsnorm_t",
    )(x, p_t, g.reshape(1, d))


def _residual_t_kernel(x_ref, pt_ref, o_ref):
    o_ref[...] = x_ref[...] + pt_ref[...].T


def residual_t(x, p_t, tm=256):
    n, d = x.shape
    tm = _tile(n, tm)
    return pl.pallas_call(
        _residual_t_kernel,
        out_shape=jax.ShapeDtypeStruct((n, d), F32),
        grid=(n // tm,),
        in_specs=[pl.BlockSpec((tm, d), lambda i: (i, 0)),
                  pl.BlockSpec((d, tm), lambda i: (0, i))],
        out_specs=pl.BlockSpec((tm, d), lambda i: (i, 0)),
        compiler_params=_params(("parallel",), 2 * 3 * tm * d * 4 + tm * d * 4),
        name="residual_t",
    )(x, p_t)


def _layer(x, hist, lw, chunk, want_v):
    b, t, d = x.shape
    n = b * t
    d_a = lw["conv_w"].shape[1]
    d_b = lw["ln_b_g"].shape[0]
    x2 = x.reshape(n, d)
    xn = rmsnorm_cast(x2, lw["norm_mix_g"])
    a = glu_proj(xn, lw["w_in"], d_a)
    uv = proj(xn, lw["w_in"], 2 * d_a, 2 * d_b)
    gates = proj(xn, lw["w_in"], 2 * d_a + 2 * d_b, 2 * d, bias=lw["b_gate"])
    act = conv_ln(a.reshape(b, t, d_a), hist, lw["conv_w"], lw["conv_b"], lw["ln_a_g"], lw["ln_a_b"]).reshape(n, d_a)
    p, *v = gmlp_mix(uv, lw["ln_b_g"], lw["ln_b_b"], lw["w_spatial"], lw["b_spatial"], chunk,
                     emit_v=want_v)
    m = merge_proj(act, p, gates, lw["w_a_out"], lw["w_b_out"])
    x1 = residual_proj(m, lw["w_o"], x2)
    xn_t, q_t = norm_query(x1, lw["norm_ffn_g"], lw["w_query_t"])
    s1_t, s2_t, thr, cst = peer_topk(q_t, lw["sub_keys"])
    peer_t = peer_dense(xn_t, s1_t, s2_t, thr, cst, lw["expert_u"], lw["expert_v_t"])
    return x1, peer_t, a.reshape(b, t, d_a), (v[0].reshape(b, t, d_b) if v else None)


def kernel(x_prompt, x_sample, cache_conv, norm_mix_g, w_in, b_gate, conv_w, conv_b, ln_a_g, ln_a_b, w_a_out, ln_b_g, ln_b_b, w_spatial, b_spatial, w_b_out, w_o, norm_ffn_g, w_query, sub_keys, expert_u, expert_v, norm_final_g):
    depth = w_in.shape[0]
    kw = conv_w.shape[1]
    d_a = conv_w.shape[2]
    gmlp_chunk = w_spatial.shape[2]
    xp, xs = x_prompt, x_sample
    bp, tp, d = xp.shape
    bs, ts, _ = xs.shape
    assert tp % gmlp_chunk == 0 and ts <= gmlp_chunk and ts % STREAM_CHUNK == 0
    assert tp >= kw - 1 and ts >= kw - 1
    conv_p, conv_s, v_s = [], [], []
    for l in range(depth):
        lw = dict(
            norm_mix_g=norm_mix_g[l], w_in=w_in[l].astype(BF16), b_gate=b_gate[l],
            conv_w=conv_w[l], conv_b=conv_b[l], ln_a_g=ln_a_g[l], ln_a_b=ln_a_b[l],
            w_a_out=w_a_out[l].astype(BF16), ln_b_g=ln_b_g[l], ln_b_b=ln_b_b[l],
            w_spatial=w_spatial[l], b_spatial=b_spatial[l], w_b_out=w_b_out[l].astype(BF16),
            w_o=w_o[l].astype(BF16), norm_ffn_g=norm_ffn_g[l],
            w_query_t=w_query[l].T.astype(BF16), sub_keys=sub_keys[l].astype(BF16),
            expert_u=expert_u[l].astype(BF16), expert_v_t=expert_v[l].T.astype(BF16))
        last = l == depth - 1
        hist_p = jnp.zeros((bp, HALO_ROWS, d_a), F32)
        hist_s = jnp.pad(cache_conv[l].astype(F32), ((0, 0), (HALO_ROWS - (kw - 1), 0), (0, 0)))
        outs = []
        for x, hist, chunk, want_v in ((xp, hist_p, gmlp_chunk, False), (xs, hist_s, ts, True)):
            b, t, _ = x.shape
            x1, peer_t, a, v = _layer(x, hist, lw, chunk, want_v)
            if last:
                y = residual_rmsnorm_t(x1, peer_t, norm_final_g)
            else:
                y = residual_t(x1, peer_t)
            outs.append((y.reshape(b, t, d), a, v))
        (xp, a_p, _), (xs, a_s, v_new) = outs
        conv_p.append(a_p[:, tp - (kw - 1):])
        conv_s.append(a_s[:, ts - (kw - 1):])
        v_s.append(v_new)
    return (xp, xs, jnp.stack(conv_p, axis=0), jnp.stack(conv_s, axis=0), jnp.stack(v_s, axis=0))
```

```python
import functools
import math

import jax
import jax.numpy as jnp
from jax import lax
from jax.experimental import pallas as pl
from jax.experimental.pallas import tpu as pltpu

F32 = jnp.float32
BF16 = jnp.bfloat16

EPS = 1e-6
STREAM_CHUNK = 64
TOPK = 16
SUBLANES = 8
LANES = 128
HALO_ROWS = 32
VMEM_LIMIT_CAP = 60000 * 1024
SQRT_HALF = 1.0 / math.sqrt(2.0)


def _tile(n, pref):
    t = min(n, pref)
    while n % t:
        t //= 2
    return t


def _params(semantics, vmem_bytes):
    limit = int(min(VMEM_LIMIT_CAP, max(vmem_bytes * 5 // 4 + (4 << 20), 16 << 20)))
    return pltpu.CompilerParams(dimension_semantics=semantics, vmem_limit_bytes=limit)


def _sorting_network(n):
    pairs = []
    p = 1
    while p < n:
        k = p
        while k >= 1:
            for j in range(k % p, n - k, 2 * k):
                for i in range(min(k, n - j - k)):
                    if (i + j) // (2 * p) == (i + j + k) // (2 * p):
                        pairs.append((i + j, i + j + k))
            k //= 2
        p *= 2
    return pairs


SORT16 = _sorting_network(TOPK)


def _transpose_cast_kernel(x_ref, o_ref):
    o_ref[...] = x_ref[...].T.astype(o_ref.dtype)


def transpose_cast(x, dtype, tr=512):
    r, c = x.shape
    tr = _tile(r, tr)
    vmem = 2 * (tr * c * x.dtype.itemsize + c * tr * jnp.dtype(dtype).itemsize) + tr * c * x.dtype.itemsize
    return pl.pallas_call(
        _transpose_cast_kernel,
        out_shape=jax.ShapeDtypeStruct((c, r), dtype),
        grid=(r // tr,),
        in_specs=[pl.BlockSpec((tr, c), lambda i: (i, 0))],
        out_specs=pl.BlockSpec((c, tr), lambda i: (0, i)),
        compiler_params=_params(("parallel",), vmem),
        name="transpose_cast",
    )(x)


def _rmsnorm_kernel(x_ref, g_ref, o_ref):
    x = x_ref[...]
    r = lax.rsqrt(jnp.mean(x * x, axis=-1, keepdims=True) + EPS)
    o_ref[...] = ((x * r) * g_ref[...]).astype(o_ref.dtype)


def rmsnorm_cast(x, g, tm=512):
    n, d = x.shape
    tm = _tile(n, tm)
    return pl.pallas_call(
        _rmsnorm_kernel,
        out_shape=jax.ShapeDtypeStruct((n, d), BF16),
        grid=(n // tm,),
        in_specs=[pl.BlockSpec((tm, d), lambda i: (i, 0)),
                  pl.BlockSpec((1, d), lambda i: (0, 0))],
        out_specs=pl.BlockSpec((tm, d), lambda i: (i, 0)),
        compiler_params=_params(("parallel",), 2 * tm * d * 6),
        name="rmsnorm_cast",
    )(x, g.reshape(1, d))


def _glu_kernel(x_ref, wl_ref, wg_ref, o_ref):
    x = x_ref[...]
    lin = jnp.dot(x, wl_ref[...], preferred_element_type=F32)
    gate = jnp.dot(x, wg_ref[...], preferred_element_type=F32)
    o_ref[...] = lin * jax.nn.sigmoid(gate)


def glu_proj(xn, w, d_a, tm=1024, tn=512):
    n, d = xn.shape
    tm, tn = _tile(n, tm), _tile(d_a, tn)
    nb = d_a // tn
    vmem = 2 * (tm * d * 2 + 2 * d * tn * 2 + tm * tn * 4)
    return pl.pallas_call(
        _glu_kernel,
        out_shape=jax.ShapeDtypeStruct((n, d_a), F32),
        grid=(n // tm, nb),
        in_specs=[pl.BlockSpec((tm, d), lambda i, j: (i, 0)),
                  pl.BlockSpec((d, tn), lambda i, j: (0, j)),
                  pl.BlockSpec((d, tn), lambda i, j: (0, j + nb))],
        out_specs=pl.BlockSpec((tm, tn), lambda i, j: (i, j)),
        compiler_params=_params(("parallel", "arbitrary"), vmem),
        name="glu_proj",
    )(xn, w, w)


def _proj_kernel(x_ref, w_ref, o_ref):
    o_ref[...] = jnp.dot(x_ref[...], w_ref[...], preferred_element_type=F32)


def _proj_gate_kernel(x_ref, w_ref, b_ref, o_ref):
    h = jnp.dot(x_ref[...], w_ref[...], preferred_element_type=F32)
    o_ref[...] = jax.nn.sigmoid(h + b_ref[...])


def proj(xn, w, col0, ncols, bias=None, tm=1024, tn=1024):
    n, d = xn.shape
    tm, tn = _tile(n, tm), _tile(math.gcd(ncols, col0) if col0 else ncols, tn)
    off = col0 // tn
    vmem = 2 * (tm * d * 2 + d * tn * 2 + tm * tn * 4)
    in_specs = [pl.BlockSpec((tm, d), lambda i, j: (i, 0)),
                pl.BlockSpec((d, tn), lambda i, j: (0, j + off))]
    args = [xn, w]
    kern = _proj_kernel
    if bias is not None:
        in_specs.append(pl.BlockSpec((1, tn), lambda i, j: (0, j)))
        args.append(bias.reshape(1, ncols))
        kern = _proj_gate_kernel
    return pl.pallas_call(
        kern,
        out_shape=jax.ShapeDtypeStruct((n, ncols), F32),
        grid=(n // tm, ncols // tn),
        in_specs=in_specs,
        out_specs=pl.BlockSpec((tm, tn), lambda i, j: (i, j)),
        compiler_params=_params(("parallel", "arbitrary"), vmem),
        name="proj_gate" if bias is not None else "proj",
    )(*args)


def _layernorm(x, g, b):
    mu = jnp.mean(x, axis=-1, keepdims=True)
    xc = x - mu
    var = jnp.mean(xc * xc, axis=-1, keepdims=True)
    return (xc * lax.rsqrt(var + EPS)) * g + b


def _conv_act_rows(window, cw_ref, cb_ref, g_ref, b_ref, cbuf_ref, *, kw, cols):
    rows, d_a = cbuf_ref.shape
    lead = HALO_ROWS - (kw - 1)
    span = rows + HALO_ROWS
    for c in range(d_a // cols):
        cs = slice(c * cols, (c + 1) * cols)
        win = window(cs)
        acc = jnp.broadcast_to(cb_ref[:, cs], (rows, cols))
        for s in range(SUBLANES):
            shifted = win if s == 0 else pltpu.roll(win, span - s, 0)
            for off in range(s, HALO_ROWS + 1, SUBLANES):
                k = off - lead
                if 0 <= k < kw:
                    acc = acc + shifted[off - s:off - s + rows, :] * cw_ref[k:k + 1, cs]
        cbuf_ref[:, cs] = acc
    y =_layernorm(cbuf_ref[...], g_ref[...], b_ref[...])
    return y * jax.nn.sigmoid(y)


def _conv_kernel(a_ref, prev_ref, hist_ref, cw_ref, cb_ref, g_ref, b_ref, o_ref,
                 buf_ref, cbuf_ref, *, tt, kw, cols):
    t = pl.program_id(1)
    buf_ref[0:HALO_ROWS, :] = jnp.where(t == 0, hist_ref[0], prev_ref[0])
    buf_ref[HALO_ROWS:HALO_ROWS + tt, :] = a_ref[0]
    rows = cbuf_ref.shape[0]

    def row_chunk(rc, carry):
        r0 = pl.multiple_of(rc * rows, rows)
        act = _conv_act_rows(lambda cs: buf_ref[pl.ds(r0, rows + HALO_ROWS), cs],
                             cw_ref, cb_ref, g_ref, b_ref, cbuf_ref, kw=kw, cols=cols)
        o_ref[0, pl.ds(r0, rows), :] = act.astype(o_ref.dtype)
        return carry

    lax.fori_loop(0, tt // rows, row_chunk, 0)


def conv_ln(a, hist, conv_w, conv_b, ln_g, ln_b, tt=512):
    b, t, d_a = a.shape
    kw = conv_w.shape[0]
    assert kw - 1 <= HALO_ROWS and t % HALO_ROWS == 0
    tt = _tile(t, tt)
    per = tt // HALO_ROWS
    rows = _tile(tt, 64)
    vmem = 2 * (tt * d_a * 4 + 2 * HALO_ROWS * d_a * 4 + tt * d_a * 2) + (tt + HALO_ROWS + rows) * d_a * 4
    vec = lambda v: v.reshape(1, d_a)
    return pl.pallas_call(
        functools.partial(_conv_kernel, tt=tt, kw=kw, cols=LANES),
        out_shape=jax.ShapeDtypeStruct((b, t, d_a), BF16),
        grid=(b, t // tt),
        in_specs=[pl.BlockSpec((1, tt, d_a), lambda i, j: (i, j, 0)),
                  pl.BlockSpec((1, HALO_ROWS, d_a), lambda i, j: (i, jnp.maximum(j * per - 1, 0), 0)),
                  pl.BlockSpec((1, HALO_ROWS, d_a), lambda i, j: (i, 0, 0)),
                  pl.BlockSpec((kw, d_a), lambda i, j: (0, 0)),
                  pl.BlockSpec((1, d_a), lambda i, j: (0, 0)),
                  pl.BlockSpec((1, d_a), lambda i, j: (0, 0)),
                  pl.BlockSpec((1, d_a), lambda i, j: (0, 0))],
        out_specs=pl.BlockSpec((1, tt, d_a), lambda i, j: (i, j, 0)),
        scratch_shapes=[pltpu.VMEM((HALO_ROWS + tt, d_a), F32),
                        pltpu.VMEM((rows, d_a), F32)],
        compiler_params=_params(("parallel", "arbitrary"), vmem),
        name="conv_ln",
    )(a, a, hist, conv_w, vec(conv_b), vec(ln_g), vec(ln_b))


def _gmlp_kernel(uv_ref, g_ref, b_ref, ws_ref, bs_ref, p_ref, *v_refs, chunk, d_b):
    groups = ws_ref.shape[0]
    gc = d_b // groups
    tg = uv_ref.shape[0]
    v = _layernorm(uv_ref[:, d_b:], g_ref[...], b_ref[...])
    for r in v_refs:
        r[...] = v
    v_ref = v_refs[-1]
    row = lax.broadcasted_iota(jnp.int32, (chunk, chunk), 0) // STREAM_CHUNK
    col = lax.broadcasted_iota(jnp.int32, (chunk, chunk), 1) // STREAM_CHUNK
    causal = row >= col
    for g in range(groups):
        ws = jnp.where(causal, ws_ref[g], 0.0).astype(BF16)
        bias = bs_ref[g]
        for c in range(tg // chunk):
            rs = slice(c * chunk, (c + 1) * chunk)
            cs = slice(g * gc, (g + 1) * gc)
            z = jnp.dot(ws, v_ref[rs, cs].astype(BF16), preferred_element_type=F32) + bias
            p_ref[rs, cs] = (uv_ref[rs, cs] * z).astype(p_ref.dtype)


def gmlp_mix(uv, ln_g, ln_b, w_spatial, b_spatial, chunk, emit_v, tg=512):
    n, two_db = uv.shape
    d_b = two_db // 2
    groups = w_spatial.shape[0]
    ws = w_spatial[:, :chunk, :chunk]
    bs = b_spatial[:, :chunk].reshape(groups, chunk, 1)
    tg = max(_tile(n, tg), chunk)
    assert n % tg == 0 and tg % chunk == 0
    vmem = 2 * (tg * two_db * 4 + tg * d_b * 6) + 2 * groups * chunk * (chunk + LANES) * 4 + tg * d_b * 4
    row_block = pl.BlockSpec((tg, d_b), lambda i: (i, 0))
    return pl.pallas_call(
        functools.partial(_gmlp_kernel, chunk=chunk, d_b=d_b),
        out_shape=(jax.ShapeDtypeStruct((n, d_b), BF16),) + ((jax.ShapeDtypeStruct((n, d_b), F32),) if emit_v else ()),
        grid=(n // tg,),
        in_specs=[pl.BlockSpec((tg, two_db), lambda i: (i, 0)),
                  pl.BlockSpec((1, d_b), lambda i: (0, 0)),
                  pl.BlockSpec((1, d_b), lambda i: (0, 0)),
                  pl.BlockSpec((groups, chunk, chunk), lambda i: (0, 0, 0)),
                  pl.BlockSpec((groups, chunk, 1), lambda i: (0, 0, 0))],
        out_specs=(row_block,) + ((row_block,) if emit_v else ()),
        scratch_shapes=[pltpu.VMEM((tg, d_b), F32)],
        compiler_params=_params(("parallel",), vmem),
        name="gmlp_mix",
    )(uv, ln_g.reshape(1, d_b), ln_b.reshape(1, d_b), ws, bs)


def _merge_kernel(act_ref, p_ref, wa_ref, wb_ref, ga_ref, gb_ref, o_ref):
    ya = jnp.dot(act_ref[...], wa_ref[...], preferred_element_type=F32)
    yb = jnp.dot(p_ref[...], wb_ref[...], preferred_element_type=F32)
    o_ref[...] = (ga_ref[...] * ya + gb_ref[...] * yb).astype(o_ref.dtype)


def merge_proj(act, p, gates, wa, wb, tm=1024, tn=512):
    n, d_a = act.shape
    d_b = p.shape[1]
    d = wa.shape[1]
    tm, tn = _tile(n, tm), _tile(d, tn)
    nb = d // tn
    vmem = 2 * (tm * (d_a + d_b) * 2 + (d_a + d_b) * tn * 2 + 2 * tm * tn * 4 + tm * tn * 2)
    return pl.pallas_call(
        _merge_kernel,
        out_shape=jax.ShapeDtypeStruct((n, d), BF16),
        grid=(n // tm, nb),
        in_specs=[pl.BlockSpec((tm, d_a), lambda i, j: (i, 0)),
                  pl.BlockSpec((tm, d_b), lambda i, j: (i, 0)),
                  pl.BlockSpec((d_a, tn), lambda i, j: (0, j)),
                  pl.BlockSpec((d_b, tn), lambda i, j: (0, j)),
                  pl.BlockSpec((tm, tn), lambda i, j: (i, j)),
                  pl.BlockSpec((tm, tn), lambda i, j: (i, j + nb))],
        out_specs=pl.BlockSpec((tm, tn), lambda i, j: (i, j)),
        compiler_params=_params(("parallel", "arbitrary"), vmem),
        name="merge_proj",
    )(act, p, wa, wb, gates, gates)


def _residual_proj_kernel(m_ref, w_ref, x_ref, o_ref):
    o_ref[...] = x_ref[...] + jnp.dot(m_ref[...], w_ref[...], preferred_element_type=F32)


def residual_proj(m, w, x, tm=1024, tn=1024):
    n, k = m.shape
    d = w.shape[1]
    tm, tn = _tile(n, tm), _tile(d, tn)
    vmem = 2 * (tm * k * 2 + k * tn * 2 + 2 * tm * tn * 4)
    return pl.pallas_call(
        _residual_proj_kernel,
        out_shape=jax.ShapeDtypeStruct((n, d), F32),
        grid=(n // tm, d // tn),
        in_specs=[pl.BlockSpec((tm, k), lambda i, j: (i, 0)),
                  pl.BlockSpec((k, tn), lambda i, j: (0, j)),
                  pl.BlockSpec((tm, tn), lambda i, j: (i, j))],
        out_specs=pl.BlockSpec((tm, tn), lambda i, j: (i, j)),
        compiler_params=_params(("parallel", "arbitrary"), vmem),
        name="residual_proj",
    )(m, w, x)


def _norm_query_kernel(x_ref, g_ref, wq_ref, xt_ref, qt_ref):
    x = x_ref[...]
    r = lax.rsqrt(jnp.mean(x * x, axis=-1, keepdims=True) + EPS)
    xt_ref[...] = ((x * r) * g_ref[...]).T.astype(xt_ref.dtype)
    qt_ref[...] = jnp.dot(wq_ref[...], xt_ref[...], preferred_element_type=F32).astype(qt_ref.dtype)


def norm_query(x, g, wq_t, tm=512):
    n, d = x.shape
    qd = wq_t.shape[0]
    tm = _tile(n, tm)
    vmem = 2 * (tm * d * 4 + d * tm * 2 + qd * tm * 2) + qd * d * 2 + tm * d * 8
    return pl.pallas_call(
        _norm_query_kernel,
        out_shape=(jax.ShapeDtypeStruct((d, n), BF16), jax.ShapeDtypeStruct((qd, n), BF16)),
        grid=(n // tm,),
        in_specs=[pl.BlockSpec((tm, d), lambda i: (i, 0)),
                  pl.BlockSpec((1, d), lambda i: (0, 0)),
                  pl.BlockSpec((qd, d), lambda i: (0, 0), pipeline_mode=pl.Buffered(1))],
        out_specs=(pl.BlockSpec((d, tm), lambda i: (0, i)),
                   pl.BlockSpec((qd, tm), lambda i: (0, i))),
        compiler_params=_params(("parallel",), vmem),
        name="norm_query",
    )(x, g.reshape(1, d), wq_t)


def _compare_exchange(v, i, j):
    hi, lo = jnp.maximum(v[i], v[j]), jnp.minimum(v[i], v[j])
    v[i], v[j] = hi, lo


def _bitonic_merge(v):
    d = TOPK // 2
    while d >= 1:
        for i in range(TOPK):
            if not i & d:
                _compare_exchange(v, i, i + d)
        d //= 2


def _merge_top(top, other):
    out = list(top)
    for i, x in enumerate(other):
        out[TOPK - 1 - i] = jnp.maximum(top[TOPK - 1 - i], x)
    _bitonic_merge(out)
    return out


def _top_values(s):
    nk = s.shape[0]
    per_sublane = [s[SUBLANES * i:SUBLANES * (i + 1), :] for i in range(nk // SUBLANES)]
    top = None
    for base in range(0, len(per_sublane), TOPK):
        v = per_sublane[base:base + TOPK]
        for i, j in SORT16:
            _compare_exchange(v, i, j)
        top = v if top is None else _merge_top(top, v)
    shift = SUBLANES // 2
    while shift >= 1:
        top = _merge_top(top, [pltpu.roll(x, shift, 0) for x in top])
        shift //= 2
    return top


def _pair_top_values(a, b):
    split = 4
    top = [a[0] + b[j] for j in range(TOPK)]
    for i in range(1, split):
        top = _merge_top(top, [a[i] + b[j] for j in range(TOPK // (i + 1))])
    for j in range(TOPK // (split + 1)):
        top = _merge_top(top, [a[i] + b[j] for i in range(split, TOPK) if (i + 1) * (j + 1) <= TOPK])
    return top


def _topk_kernel(q_ref, k_ref, s1_ref, s2_ref, thr_ref, cst_ref, *, heads, nk, hk):
    sublane = lax.broadcasted_iota(jnp.int32, thr_ref.shape, 0)
    packed = [None, None]
    for h in range(heads):
        for p, s_ref in enumerate((s1_ref, s2_ref)):
            q = q_ref[(2 * h + p) * hk:(2 * h + p + 1) * hk, :]
            s = jnp.dot(k_ref[h, p], q, preferred_element_type=F32)
            s_ref[h * nk:(h + 1) * nk, :] = s
            vals = _top_values(s)
            packed[p] = vals if h == 0 else [jnp.where(sublane == h, v, acc) for v, acc in zip(vals, packed[p])]
    top = _pair_top_values(packed[0], packed[1])
    z = jnp.exp(top[0] - top[0])
    for k in range(1, TOPK):
        z = z + jnp.exp(top[k] - top[0])
    thr_ref[...] = top[TOPK - 1]
    cst_ref[...] = top[0] + jnp.log(z)


def peer_topk(q_t, keys, tn=256):
    qd, n = q_t.shape
    heads, _, nk, hk = keys.shape
    assert nk % (SUBLANES * TOPK) == 0 and heads == SUBLANES
    tn = _tile(n, tn)
    vmem = 2 * (qd * tn * 2 + keys.size * 2 + 2 * heads * nk * tn * 4 + 2 * SUBLANES * tn * 4)
    return pl.pallas_call(
        functools.partial(_topk_kernel, heads=heads, nk=nk, hk=hk),
        out_shape=(jax.ShapeDtypeStruct((heads * nk, n), F32),
                   jax.ShapeDtypeStruct((heads * nk, n), F32),
                   jax.ShapeDtypeStruct((heads, n), F32),
                   jax.ShapeDtypeStruct((heads, n), F32)),
        grid=(n // tn,),
        in_specs=[pl.BlockSpec((qd, tn), lambda i: (0, i)),
                  pl.BlockSpec((heads, 2, nk, hk), lambda i: (0, 0, 0, 0))],
        out_specs=(pl.BlockSpec((heads * nk, tn), lambda i: (0, i)),
                   pl.BlockSpec((heads * nk, tn), lambda i: (0, i)),
                   pl.BlockSpec((heads, tn), lambda i: (0, i)),
                   pl.BlockSpec((heads, tn), lambda i: (0, i))),
        compiler_params=_params(("parallel",), vmem),
        name="peer_topk",
    )(q_t, keys)


def _peer_kernel(xt_ref, s1_ref, s2_ref, thr_ref, cst_ref, u_ref, vt_ref, o_ref,
                 h_ref, w_ref, *, heads, nk, lane_chunk):
    j = pl.program_id(1)
    last = pl.num_programs(1) - 1
    te, tm = w_ref.shape
    per = te // nk
    chunks = [slice(c * lane_chunk, (c + 1) * lane_chunk) for c in range(tm // lane_chunk)]

    def weights(ls):
        for ii in range(per):
            row1 = (j - 1) * per + ii
            gate = None
            for h in range(heads):
                s = s2_ref[h * nk:(h + 1) * nk, ls] + s1_ref[pl.ds(h * nk + row1, 1), ls]
                keep = s >= thr_ref[h:h + 1, ls]
                term = jnp.where(keep, jnp.exp(s - cst_ref[h:h + 1, ls]), 0.0)
                gate = term if gate is None else gate + term
            hh = h_ref[ii * nk:(ii + 1) * nk, ls]
            gelu = 0.5 * hh * (1.0 + lax.erf(hh * SQRT_HALF))
            w_ref[ii * nk:(ii + 1) * nk, ls] = (gate * gelu).astype(w_ref.dtype)

    def hidden(ls):
        return jnp.dot(u_ref[...], xt_ref[:, ls], preferred_element_type=F32)

    def accumulate(ls):
        o_ref[:, ls] += jnp.dot(vt_ref[...], w_ref[:, ls], preferred_element_type=F32)

    @pl.when(j == 0)
    def _():
        o_ref[...] = jnp.zeros_like(o_ref)
        for ls in chunks:
            h_ref[:, ls] = hidden(ls)

    @pl.when(jnp.logical_and(j > 0, j < last))
    def _():
        for ls in chunks:
            weights(ls)
            h_ref[:, ls] = hidden(ls)
            accumulate(ls)

    @pl.when(j == last)
    def _():
        for ls in chunks:
            weights(ls)
            accumulate(ls)


def peer_dense(xn_t, s1_t, s2_t, thr, cst, u, v_t, tm=1024, te=512):
    d, n = xn_t.shape
    ne = u.shape[0]
    heads = thr.shape[0]
    nk = s1_t.shape[0] // heads
    assert ne == nk * nk
    tm, te = _tile(n, tm), max(_tile(ne, te), nk)
    nt = ne // te
    once = pl.Buffered(1)
    vmem = (d * tm * 2 + 2 * heads * nk * tm * 4 + 2 * SUBLANES * tm * 4 + d * tm * 4
            + 2 * 2 * te * d * 2 + te * tm * 6)
    return pl.pallas_call(
        functools.partial(_peer_kernel, heads=heads, nk=nk, lane_chunk=_tile(tm, 2 * LANES)),
        out_shape=jax.ShapeDtypeStruct((d, n), F32),
        grid=(n // tm, nt + 1),
        in_specs=[pl.BlockSpec((d, tm), lambda i, j: (0, i), pipeline_mode=once),
                  pl.BlockSpec((heads * nk, tm), lambda i, j: (0, i), pipeline_mode=once),
                  pl.BlockSpec((heads * nk, tm), lambda i, j: (0, i), pipeline_mode=once),
                  pl.BlockSpec((heads, tm), lambda i, j: (0, i), pipeline_mode=once),
                  pl.BlockSpec((heads, tm), lambda i, j: (0, i), pipeline_mode=once),
                  pl.BlockSpec((te, d), lambda i, j: (jnp.minimum(j, nt - 1), 0)),
                  pl.BlockSpec((d, te), lambda i, j: (0, jnp.maximum(j - 1, 0)))],
        out_specs=pl.BlockSpec((d, tm), lambda i, j: (0, i), pipeline_mode=once),
        scratch_shapes=[pltpu.VMEM((te, tm), F32), pltpu.VMEM((te, tm), BF16)],
        compiler_params=_params(("parallel", "arbitrary"), vmem),
        name="peer_dense",
    )(xn_t, s1_t, s2_t, thr, cst, u, v_t)


def _final_kernel(x_ref, pt_ref, g_ref, o_ref):
    x = x_ref[...] + pt_ref[...].T
    r = lax.rsqrt(jnp.mean(x * x, axis=-1, keepdims=True) + EPS)
    o_ref[...] = (x * r) * g_ref[...]


def residual_rmsnorm_t(x, p_t, g, tm=256):
    n, d = x.shape
    tm = _tile(n, tm)
    return pl.pallas_call(
        _final_kernel,
        out_shape=jax.ShapeDtypeStruct((n, d), F32),
        grid=(n // tm,),
        in_specs=[pl.BlockSpec((tm, d), lambda i: (i, 0)),
                  pl.BlockSpec((d, tm), lambda i: (0, i)),
                  pl.BlockSpec((1, d), lambda i: (0, 0))],
        out_specs=pl.BlockSpec((tm, d), lambda i: (i, 0)),
        compiler_params=_params(("parallel",), 2 * 3 * tm * d * 4 + tm * d * 4),
        name="residual_rmsnorm_t",
    )(x, p_t, g.reshape(1, d))


def _residual_t_kernel(x_ref, pt_ref, o_ref):
    o_ref[...] = x_ref[...] + pt_ref[...].T


def residual_t(x, p_t, tm=256):
    n, d = x.shape
    tm = _tile(n, tm)
    return pl.pallas_call(
        _residual_t_kernel,
        out_shape=jax.ShapeDtypeStruct((n, d), F32),
        grid=(n // tm,),
        in_specs=[pl.BlockSpec((tm, d), lambda i: (i, 0)),
                  pl.BlockSpec((d, tm), lambda i: (0, i))],
        out_specs=pl.BlockSpec((tm, d), lambda i: (i, 0)),
        compiler_params=_params(("parallel",), 2 * 3 * tm * d * 4 + tm * d * 4),
        name="residual_t",
    )(x, p_t)


def _layer(x, hist, lw, chunk, want_v):
    b, t, d = x.shape
    n = b * t
    d_a = lw["conv_w"].shape[1]
    d_b = lw["ln_b_g"].shape[0]
    x2 = x.reshape(n, d)
    xn = rmsnorm_cast(x2, lw["norm_mix_g"])
    a = glu_proj(xn, lw["w_in"], d_a)
    uv = proj(xn, lw["w_in"], 2 * d_a, 2 * d_b)
    gates = proj(xn, lw["w_in"], 2 * d_a + 2 * d_b, 2 * d, bias=lw["b_gate"])
    act = conv_ln(a.reshape(b, t, d_a), hist, lw["conv_w"], lw["conv_b"], lw["ln_a_g"], lw["ln_a_b"]).reshape(n, d_a)
    p, *v = gmlp_mix(uv, lw["ln_b_g"], lw["ln_b_b"], lw["w_spatial"], lw["b_spatial"], chunk,
                     emit_v=want_v)
    m = merge_proj(act, p, gates, lw["w_a_out"], lw["w_b_out"])
    x1 = residual_proj(m, lw["w_o"], x2)
    xn_t, q_t = norm_query(x1, lw["norm_ffn_g"], lw["w_query_t"])
    s1_t, s2_t, thr, cst = peer_topk(q_t, lw["sub_keys"])
    peer_t = peer_dense(xn_t, s1_t, s2_t, thr, cst, lw["expert_u"], lw["expert_v_t"])
    return x1, peer_t, a.reshape(b, t, d_a), (v[0].reshape(b, t, d_b) if v else None)


def kernel(x_prompt, x_sample, cache_conv, norm_mix_g, w_in, b_gate, conv_w, conv_b, ln_a_g, ln_a_b, w_a_out, ln_b_g, ln_b_b, w_spatial, b_spatial, w_b_out, w_o, norm_ffn_g, w_query, sub_keys, expert_u, expert_v, norm_final_g):
    depth = w_in.shape[0]
    kw = conv_w.shape[1]
    d_a = conv_w.shape[2]
    gmlp_chunk = w_spatial.shape[2]
    xp, xs = x_prompt, x_sample
    bp, tp, d = xp.shape
    bs, ts, _ = xs.shape
    assert tp % gmlp_chunk == 0 and ts <= gmlp_chunk and ts % STREAM_CHUNK == 0
    assert tp >= kw - 1 and ts >= kw - 1
    conv_p, conv_s, v_s = [], [], []
    for l in range(depth):
        lw = dict(
            norm_mix_g=norm_mix_g[l], w_in=w_in[l].astype(BF16), b_gate=b_gate[l],
            conv_w=conv_w[l], conv_b=conv_b[l], ln_a_g=ln_a_g[l], ln_a_b=ln_a_b[l],
            w_a_out=w_a_out[l].astype(BF16), ln_b_g=ln_b_g[l], ln_b_b=ln_b_b[l],
            w_spatial=w_spatial[l], b_spatial=b_spatial[l], w_b_out=w_b_out[l].astype(BF16),
            w_o=w_o[l].astype(BF16), norm_ffn_g=norm_ffn_g[l],
            w_query_t=transpose_cast(w_query[l], BF16), sub_keys=sub_keys[l].astype(BF16),
            expert_u=expert_u[l].astype(BF16), expert_v_t=transpose_cast(expert_v[l], BF16))
        last = l == depth - 1
        hist_p = jnp.zeros((bp, HALO_ROWS, d_a), F32)
        hist_s = jnp.pad(cache_conv[l].astype(F32), ((0, 0), (HALO_ROWS - (kw - 1), 0), (0, 0)))
        outs = []
        for x, hist, chunk, want_v in ((xp, hist_p, gmlp_chunk, False), (xs, hist_s, ts, True)):
            b, t, _ = x.shape
            x1, peer_t, a, v = _layer(x, hist, lw, chunk, want_v)
            if last:
                y = residual_rmsnorm_t(x1, peer_t, norm_final_g)
            else:
                y = residual_t(x1, peer_t)
            outs.append((y.reshape(b, t, d), a, v))
        (xp, a_p, _), (xs, a_s, v_new) = outs
        conv_p.append(a_p[:, tp - (kw - 1):])
        conv_s.append(a_s[:, ts - (kw - 1):])
        v_s.append(v_new)
    return (xp, xs, jnp.stack(conv_p, axis=0), jnp.stack(conv_s, axis=0), jnp.stack(v_s, axis=0))
```

```python
import functools
import math

import jax
import jax.numpy as jnp
from jax import lax
from jax.experimental import pallas as pl
from jax.experimental.pallas import tpu as pltpu

F32 = jnp.float32
BF16 = jnp.bfloat16

EPS = 1e-6
STREAM_CHUNK = 64
TOPK = 16
SUBLANES = 8
LANES = 128
HALO_ROWS = 32
VMEM_LIMIT_CAP = 60000 * 1024
SQRT_HALF = 1.0 / math.sqrt(2.0)


def _tile(n, pref):
    t = min(n, pref)
    while n % t:
        t //= 2
    return t


def _params(semantics, vmem_bytes):
    limit = int(min(VMEM_LIMIT_CAP, max(vmem_bytes * 5 // 4 + (4 << 20), 16 << 20)))
    return pltpu.CompilerParams(dimension_semantics=semantics, vmem_limit_bytes=limit)


def _sorting_network(n):
    pairs = []
    p = 1
    while p < n:
        k = p
        while k >= 1:
            for j in range(k % p, n - k, 2 * k):
                for i in range(min(k, n - j - k)):
                    if (i + j) // (2 * p) == (i + j + k) // (2 * p):
                        pairs.append((i + j, i + j + k))
            k //= 2
        p *= 2
    return pairs


SORT16 = _sorting_network(TOPK)


def _transpose_cast_kernel(x_ref, o_ref):
    o_ref[...] = x_ref[...].T.astype(o_ref.dtype)


def transpose_cast(x, dtype, tr=512):
    r, c = x.shape
    tr = _tile(r, tr)
    vmem = 2 * (tr * c * x.dtype.itemsize + c * tr * jnp.dtype(dtype).itemsize) + tr * c * x.dtype.itemsize
    return pl.pallas_call(
        _transpose_cast_kernel,
        out_shape=jax.ShapeDtypeStruct((c, r), dtype),
        grid=(r // tr,),
        in_specs=[pl.BlockSpec((tr, c), lambda i: (i, 0))],
        out_specs=pl.BlockSpec((c, tr), lambda i: (0, i)),
        compiler_params=_params(("parallel",), vmem),
        name="transpose_cast",
    )(x)


def _rmsnorm_kernel(x_ref, g_ref, o_ref):
    x = x_ref[...]
    r = lax.rsqrt(jnp.mean(x * x, axis=-1, keepdims=True) + EPS)
    o_ref[...] = ((x * r) * g_ref[...]).astype(o_ref.dtype)


def rmsnorm_cast(x, g, tm=512):
    n, d = x.shape
    tm = _tile(n, tm)
    return pl.pallas_call(
        _rmsnorm_kernel,
        out_shape=jax.ShapeDtypeStruct((n, d), BF16),
        grid=(n // tm,),
        in_specs=[pl.BlockSpec((tm, d), lambda i: (i, 0)),
                  pl.BlockSpec((1, d), lambda i: (0, 0))],
        out_specs=pl.BlockSpec((tm, d), lambda i: (i, 0)),
        compiler_params=_params(("parallel",), 2 * tm * d * 6),
        name="rmsnorm_cast",
    )(x, g.reshape(1, d))


def _glu_kernel(x_ref, wl_ref, wg_ref, o_ref):
    x = x_ref[...]
    lin = jnp.dot(x, wl_ref[...], preferred_element_type=F32)
    gate = jnp.dot(x, wg_ref[...], preferred_element_type=F32)
    o_ref[...] = lin * jax.nn.sigmoid(gate)


def glu_proj(xn, w, d_a, tm=1024, tn=512):
    n, d = xn.shape
    tm, tn = _tile(n, tm), _tile(d_a, tn)
    nb = d_a // tn
    vmem = 2 * (tm * d * 2 + 2 * d * tn * 2 + tm * tn * 4)
    return pl.pallas_call(
        _glu_kernel,
        out_shape=jax.ShapeDtypeStruct((n, d_a), F32),
        grid=(n // tm, nb),
        in_specs=[pl.BlockSpec((tm, d), lambda i, j: (i, 0)),
                  pl.BlockSpec((d, tn), lambda i, j: (0, j)),
                  pl.BlockSpec((d, tn), lambda i, j: (0, j + nb))],
        out_specs=pl.BlockSpec((tm, tn), lambda i, j: (i, j)),
        compiler_params=_params(("parallel", "arbitrary"), vmem),
        name="glu_proj",
    )(xn, w, w)


def _proj_kernel(x_ref, w_ref, o_ref):
    o_ref[...] = jnp.dot(x_ref[...], w_ref[...], preferred_element_type=F32)


def _proj_gate_kernel(x_ref, w_ref, b_ref, o_ref):
    h = jnp.dot(x_ref[...], w_ref[...], preferred_element_type=F32)
    o_ref[...] = jax.nn.sigmoid(h + b_ref[...]).astype(o_ref.dtype)


def proj(xn, w, col0, ncols, bias=None, tm=1024, tn=1024):
    n, d = xn.shape
    tm, tn = _tile(n, tm), _tile(math.gcd(ncols, col0) if col0 else ncols, tn)
    off = col0 // tn
    vmem = 2 * (tm * d * 2 + d * tn * 2 + tm * tn * 4)
    in_specs = [pl.BlockSpec((tm, d), lambda i, j: (i, 0)),
                pl.BlockSpec((d, tn), lambda i, j: (0, j + off))]
    args = [xn, w]
    kern = _proj_kernel
    if bias is not None:
        in_specs.append(pl.BlockSpec((1, tn), lambda i, j: (0, j)))
        args.append(bias.reshape(1, ncols))
        kern = _proj_gate_kernel
    return pl.pallas_call(
        kern,
        out_shape=jax.ShapeDtypeStruct((n, ncols), F32 if bias is None else BF16),
        grid=(n // tm, ncols // tn),
        in_specs=in_specs,
        out_specs=pl.BlockSpec((tm, tn), lambda i, j: (i, j)),
        compiler_params=_params(("parallel", "arbitrary"), vmem),
        name="proj_gate" if bias is not None else "proj",
    )(*args)


def _layernorm(x, g, b):
    mu = jnp.mean(x, axis=-1, keepdims=True)
    xc = x - mu
    var = jnp.mean(xc * xc, axis=-1, keepdims=True)
    return (xc * lax.rsqrt(var + EPS)) * g + b


def _conv_act_rows(window, cw_ref, cb_ref, g_ref, b_ref, cbuf_ref, *, kw, cols):
    rows, d_a = cbuf_ref.shape
    lead = HALO_ROWS - (kw - 1)
    span = rows + HALO_ROWS
    for c in range(d_a // cols):
        cs = slice(c * cols, (c + 1) * cols)
        win = window(cs)
        acc = jnp.broadcast_to(cb_ref[:, cs], (rows, cols))
        for s in range(SUBLANES):
            shifted = win if s == 0 else pltpu.roll(win, span - s, 0)
            for off in range(s, HALO_ROWS + 1, SUBLANES):
                k = off - lead
                if 0 <= k < kw:
                    acc = acc + shifted[off - s:off - s + rows, :] * cw_ref[k:k + 1, cs]
        cbuf_ref[:, cs] = acc
    y =_layernorm(cbuf_ref[...], g_ref[...], b_ref[...])
    return y * jax.nn.sigmoid(y)


def _conv_kernel(a_ref, prev_ref, hist_ref, cw_ref, cb_ref, g_ref, b_ref, o_ref,
                 buf_ref, cbuf_ref, *, tt, kw, cols):
    t = pl.program_id(1)
    buf_ref[0:HALO_ROWS, :] = jnp.where(t == 0, hist_ref[0], prev_ref[0])
    buf_ref[HALO_ROWS:HALO_ROWS + tt, :] = a_ref[0]
    rows = cbuf_ref.shape[0]

    def row_chunk(rc, carry):
        r0 = pl.multiple_of(rc * rows, rows)
        act = _conv_act_rows(lambda cs: buf_ref[pl.ds(r0, rows + HALO_ROWS), cs],
                             cw_ref, cb_ref, g_ref, b_ref, cbuf_ref, kw=kw, cols=cols)
        o_ref[0, pl.ds(r0, rows), :] = act.astype(o_ref.dtype)
        return carry

    lax.fori_loop(0, tt // rows, row_chunk, 0)


def conv_ln(a, hist, conv_w, conv_b, ln_g, ln_b, tt=512):
    b, t, d_a = a.shape
    kw = conv_w.shape[0]
    assert kw - 1 <= HALO_ROWS and t % HALO_ROWS == 0
    tt = _tile(t, tt)
    per = tt // HALO_ROWS
    rows = _tile(tt, 64)
    vmem = 2 * (tt * d_a * 4 + 2 * HALO_ROWS * d_a * 4 + tt * d_a * 2) + (tt + HALO_ROWS + rows) * d_a * 4
    vec = lambda v: v.reshape(1, d_a)
    return pl.pallas_call(
        functools.partial(_conv_kernel, tt=tt, kw=kw, cols=LANES),
        out_shape=jax.ShapeDtypeStruct((b, t, d_a), BF16),
        grid=(b, t // tt),
        in_specs=[pl.BlockSpec((1, tt, d_a), lambda i, j: (i, j, 0)),
                  pl.BlockSpec((1, HALO_ROWS, d_a), lambda i, j: (i, jnp.maximum(j * per - 1, 0), 0)),
                  pl.BlockSpec((1, HALO_ROWS, d_a), lambda i, j: (i, 0, 0)),
                  pl.BlockSpec((kw, d_a), lambda i, j: (0, 0)),
                  pl.BlockSpec((1, d_a), lambda i, j: (0, 0)),
                  pl.BlockSpec((1, d_a), lambda i, j: (0, 0)),
                  pl.BlockSpec((1, d_a), lambda i, j: (0, 0))],
        out_specs=pl.BlockSpec((1, tt, d_a), lambda i, j: (i, j, 0)),
        scratch_shapes=[pltpu.VMEM((HALO_ROWS + tt, d_a), F32),
                        pltpu.VMEM((rows, d_a), F32)],
        compiler_params=_params(("parallel", "arbitrary"), vmem),
        name="conv_ln",
    )(a, a, hist, conv_w, vec(conv_b), vec(ln_g), vec(ln_b))


def _gmlp_kernel(uv_ref, g_ref, b_ref, ws_ref, bs_ref, p_ref, *v_refs, chunk, d_b):
    groups = ws_ref.shape[0]
    gc = d_b // groups
    tg = uv_ref.shape[0]
    v = _layernorm(uv_ref[:, d_b:], g_ref[...], b_ref[...])
    for r in v_refs:
        r[...] = v
    v_ref = v_refs[-1]
    row = lax.broadcasted_iota(jnp.int32, (chunk, chunk), 0) // STREAM_CHUNK
    col = lax.broadcasted_iota(jnp.int32, (chunk, chunk), 1) // STREAM_CHUNK
    causal = row >= col
    for g in range(groups):
        ws = jnp.where(causal, ws_ref[g], 0.0).astype(BF16)
        bias = bs_ref[g]
        for c in range(tg // chunk):
            rs = slice(c * chunk, (c + 1) * chunk)
            cs = slice(g * gc, (g + 1) * gc)
            z = jnp.dot(ws, v_ref[rs, cs].astype(BF16), preferred_element_type=F32) + bias
            p_ref[rs, cs] = (uv_ref[rs, cs] * z).astype(p_ref.dtype)


def gmlp_mix(uv, ln_g, ln_b, w_spatial, b_spatial, chunk, emit_v, tg=512):
    n, two_db = uv.shape
    d_b = two_db // 2
    groups = w_spatial.shape[0]
    ws = w_spatial[:, :chunk, :chunk]
    bs = b_spatial[:, :chunk].reshape(groups, chunk, 1)
    tg = max(_tile(n, tg), chunk)
    assert n % tg == 0 and tg % chunk == 0
    vmem = 2 * (tg * two_db * 4 + tg * d_b * 6) + 2 * groups * chunk * (chunk + LANES) * 4 + tg * d_b * 4
    row_block = pl.BlockSpec((tg, d_b), lambda i: (i, 0))
    return pl.pallas_call(
        functools.partial(_gmlp_kernel, chunk=chunk, d_b=d_b),
        out_shape=(jax.ShapeDtypeStruct((n, d_b), BF16),) + ((jax.ShapeDtypeStruct((n, d_b), F32),) if emit_v else ()),
        grid=(n // tg,),
        in_specs=[pl.BlockSpec((tg, two_db), lambda i: (i, 0)),
                  pl.BlockSpec((1, d_b), lambda i: (0, 0)),
                  pl.BlockSpec((1, d_b), lambda i: (0, 0)),
                  pl.BlockSpec((groups, chunk, chunk), lambda i: (0, 0, 0)),
                  pl.BlockSpec((groups, chunk, 1), lambda i: (0, 0, 0))],
        out_specs=(row_block,) + ((row_block,) if emit_v else ()),
        scratch_shapes=[pltpu.VMEM((tg, d_b), F32)],
        compiler_params=_params(("parallel",), vmem),
        name="gmlp_mix",
    )(uv, ln_g.reshape(1, d_b), ln_b.reshape(1, d_b), ws, bs)


def _merge_kernel(act_ref, p_ref, wa_ref, wb_ref, ga_ref, gb_ref, o_ref):
    ya = jnp.dot(act_ref[...], wa_ref[...], preferred_element_type=F32)
    yb = jnp.dot(p_ref[...], wb_ref[...], preferred_element_type=F32)
    o_ref[...] = (ga_ref[...] * ya + gb_ref[...] * yb).astype(o_ref.dtype)


def merge_proj(act, p, gates, wa, wb, tm=1024, tn=512):
    n, d_a = act.shape
    d_b = p.shape[1]
    d = wa.shape[1]
    tm, tn = _tile(n, tm), _tile(d, tn)
    nb = d // tn
    vmem = 2 * (tm * (d_a + d_b) * 2 + (d_a + d_b) * tn * 2 + 2 * tm * tn * 4 + tm * tn * 2)
    return pl.pallas_call(
        _merge_kernel,
        out_shape=jax.ShapeDtypeStruct((n, d), BF16),
        grid=(n // tm, nb),
        in_specs=[pl.BlockSpec((tm, d_a), lambda i, j: (i, 0)),
                  pl.BlockSpec((tm, d_b), lambda i, j: (i, 0)),
                  pl.BlockSpec((d_a, tn), lambda i, j: (0, j)),
                  pl.BlockSpec((d_b, tn), lambda i, j: (0, j)),
                  pl.BlockSpec((tm, tn), lambda i, j: (i, j)),
                  pl.BlockSpec((tm, tn), lambda i, j: (i, j + nb))],
        out_specs=pl.BlockSpec((tm, tn), lambda i, j: (i, j)),
        compiler_params=_params(("parallel", "arbitrary"), vmem),
        name="merge_proj",
    )(act, p, wa, wb, gates, gates)


def _residual_proj_kernel(m_ref, w_ref, x_ref, o_ref):
    o_ref[...] = x_ref[...] + jnp.dot(m_ref[...], w_ref[...], preferred_element_type=F32)


def residual_proj(m, w, x, tm=1024, tn=1024):
    n, k = m.shape
    d = w.shape[1]
    tm, tn = _tile(n, tm), _tile(d, tn)
    vmem = 2 * (tm * k * 2 + k * tn * 2 + 2 * tm * tn * 4)
    return pl.pallas_call(
        _residual_proj_kernel,
        out_shape=jax.ShapeDtypeStruct((n, d), F32),
        grid=(n // tm, d // tn),
        in_specs=[pl.BlockSpec((tm, k), lambda i, j: (i, 0)),
                  pl.BlockSpec((k, tn), lambda i, j: (0, j)),
                  pl.BlockSpec((tm, tn), lambda i, j: (i, j))],
        out_specs=pl.BlockSpec((tm, tn), lambda i, j: (i, j)),
        compiler_params=_params(("parallel", "arbitrary"), vmem),
        name="residual_proj",
    )(m, w, x)


def _norm_query_kernel(x_ref, g_ref, wq_ref, xt_ref, qt_ref):
    x = x_ref[...]
    r = lax.rsqrt(jnp.mean(x * x, axis=-1, keepdims=True) + EPS)
    xt_ref[...] = ((x * r) * g_ref[...]).T.astype(xt_ref.dtype)
    qt_ref[...] = jnp.dot(wq_ref[...], xt_ref[...], preferred_element_type=F32).astype(qt_ref.dtype)


def norm_query(x, g, wq_t, tm=512):
    n, d = x.shape
    qd = wq_t.shape[0]
    tm = _tile(n, tm)
    vmem = 2 * (tm * d * 4 + d * tm * 2 + qd * tm * 2) + qd * d * 2 + tm * d * 8
    return pl.pallas_call(
        _norm_query_kernel,
        out_shape=(jax.ShapeDtypeStruct((d, n), BF16), jax.ShapeDtypeStruct((qd, n), BF16)),
        grid=(n // tm,),
        in_specs=[pl.BlockSpec((tm, d), lambda i: (i, 0)),
                  pl.BlockSpec((1, d), lambda i: (0, 0)),
                  pl.BlockSpec((qd, d), lambda i: (0, 0), pipeline_mode=pl.Buffered(1))],
        out_specs=(pl.BlockSpec((d, tm), lambda i: (0, i)),
                   pl.BlockSpec((qd, tm), lambda i: (0, i))),
        compiler_params=_params(("parallel",), vmem),
        name="norm_query",
    )(x, g.reshape(1, d), wq_t)


def _compare_exchange(v, i, j):
    hi, lo = jnp.maximum(v[i], v[j]), jnp.minimum(v[i], v[j])
    v[i], v[j] = hi, lo


def _bitonic_merge(v):
    d = TOPK // 2
    while d >= 1:
        for i in range(TOPK):
            if not i & d:
                _compare_exchange(v, i, i + d)
        d //= 2


def _merge_top(top, other):
    out = list(top)
    for i, x in enumerate(other):
        out[TOPK - 1 - i] = jnp.maximum(top[TOPK - 1 - i], x)
    _bitonic_merge(out)
    return out


def _top_values(s):
    nk = s.shape[0]
    per_sublane = [s[SUBLANES * i:SUBLANES * (i + 1), :] for i in range(nk // SUBLANES)]
    top = None
    for base in range(0, len(per_sublane), TOPK):
        v = per_sublane[base:base + TOPK]
        for i, j in SORT16:
            _compare_exchange(v, i, j)
        top = v if top is None else _merge_top(top, v)
    shift = SUBLANES // 2
    while shift >= 1:
        top = _merge_top(top, [pltpu.roll(x, shift, 0) for x in top])
        shift //= 2
    return top


def _pair_top_values(a, b):
    split = 4
    top = [a[0] + b[j] for j in range(TOPK)]
    for i in range(1, split):
        top = _merge_top(top, [a[i] + b[j] for j in range(TOPK // (i + 1))])
    for j in range(TOPK // (split + 1)):
        top = _merge_top(top, [a[i] + b[j] for i in range(split, TOPK) if (i + 1) * (j + 1) <= TOPK])
    return top


def _topk_kernel(q_ref, k_ref, s1_ref, s2_ref, thr_ref, cst_ref, *, heads, nk, hk):
    sublane = lax.broadcasted_iota(jnp.int32, thr_ref.shape, 0)
    packed = [None, None]
    for h in range(heads):
        for p, s_ref in enumerate((s1_ref, s2_ref)):
            q = q_ref[(2 * h + p) * hk:(2 * h + p + 1) * hk, :]
            s = jnp.dot(k_ref[h, p], q, preferred_element_type=F32)
            s_ref[h * nk:(h + 1) * nk, :] = s
            vals = _top_values(s)
            packed[p] = vals if h == 0 else [jnp.where(sublane == h, v, acc) for v, acc in zip(vals, packed[p])]
    top = _pair_top_values(packed[0], packed[1])
    z = jnp.exp(top[0] - top[0])
    for k in range(1, TOPK):
        z = z + jnp.exp(top[k] - top[0])
    thr_ref[...] = top[TOPK - 1]
    cst_ref[...] = top[0] + jnp.log(z)


def peer_topk(q_t, keys, tn=256):
    qd, n = q_t.shape
    heads, _, nk, hk = keys.shape
    assert nk % (SUBLANES * TOPK) == 0 and heads == SUBLANES
    tn = _tile(n, tn)
    vmem = 2 * (qd * tn * 2 + keys.size * 2 + 2 * heads * nk * tn * 4 + 2 * SUBLANES * tn * 4)
    return pl.pallas_call(
        functools.partial(_topk_kernel, heads=heads, nk=nk, hk=hk),
        out_shape=(jax.ShapeDtypeStruct((heads * nk, n), F32),
                   jax.ShapeDtypeStruct((heads * nk, n), F32),
                   jax.ShapeDtypeStruct((heads, n), F32),
                   jax.ShapeDtypeStruct((heads, n), F32)),
        grid=(n // tn,),
        in_specs=[pl.BlockSpec((qd, tn), lambda i: (0, i)),
                  pl.BlockSpec((heads, 2, nk, hk), lambda i: (0, 0, 0, 0))],
        out_specs=(pl.BlockSpec((heads * nk, tn), lambda i: (0, i)),
                   pl.BlockSpec((heads * nk, tn), lambda i: (0, i)),
                   pl.BlockSpec((heads, tn), lambda i: (0, i)),
                   pl.BlockSpec((heads, tn), lambda i: (0, i))),
        compiler_params=_params(("parallel",), vmem),
        name="peer_topk",
    )(q_t, keys)


def _peer_kernel(xt_ref, s1_ref, s2_ref, thr_ref, cst_ref, u_ref, vt_ref, o_ref,
                 h_ref, w_ref, *, heads, nk, lane_chunk):
    j = pl.program_id(1)
    last = pl.num_programs(1) - 1
    te, tm = w_ref.shape
    per = te // nk
    chunks = [slice(c * lane_chunk, (c + 1) * lane_chunk) for c in range(tm // lane_chunk)]

    def weights(ls):
        for ii in range(per):
            row1 = (j - 1) * per + ii
            gate = None
            for h in range(heads):
                s = s2_ref[h * nk:(h + 1) * nk, ls] + s1_ref[pl.ds(h * nk + row1, 1), ls]
                keep = s >= thr_ref[h:h + 1, ls]
                term = jnp.where(keep, jnp.exp(s - cst_ref[h:h + 1, ls]), 0.0)
                gate = term if gate is None else gate + term
            hh = h_ref[ii * nk:(ii + 1) * nk, ls]
            gelu = 0.5 * hh * (1.0 + lax.erf(hh * SQRT_HALF))
            w_ref[ii * nk:(ii + 1) * nk, ls] = (gate * gelu).astype(w_ref.dtype)

    def hidden(ls):
        return jnp.dot(u_ref[...], xt_ref[:, ls], preferred_element_type=F32)

    def accumulate(ls):
        o_ref[:, ls] += jnp.dot(vt_ref[...], w_ref[:, ls], preferred_element_type=F32)

    @pl.when(j == 0)
    def _():
        o_ref[...] = jnp.zeros_like(o_ref)
        for ls in chunks:
            h_ref[:, ls] = hidden(ls)

    @pl.when(jnp.logical_and(j > 0, j < last))
    def _():
        for ls in chunks:
            weights(ls)
            h_ref[:, ls] = hidden(ls)
            accumulate(ls)

    @pl.when(j == last)
    def _():
        for ls in chunks:
            weights(ls)
            accumulate(ls)


def peer_dense(xn_t, s1_t, s2_t, thr, cst, u, v_t, tm=1024, te=512):
    d, n = xn_t.shape
    ne = u.shape[0]
    heads = thr.shape[0]
    nk = s1_t.shape[0] // heads
    assert ne == nk * nk
    tm, te = _tile(n, tm), max(_tile(ne, te), nk)
    nt = ne // te
    once = pl.Buffered(1)
    vmem = (d * tm * 2 + 2 * heads * nk * tm * 4 + 2 * SUBLANES * tm * 4 + d * tm * 4
            + 2 * 2 * te * d * 2 + te * tm * 6)
    return pl.pallas_call(
        functools.partial(_peer_kernel, heads=heads, nk=nk, lane_chunk=_tile(tm, 2 * LANES)),
        out_shape=jax.ShapeDtypeStruct((d, n), F32),
        grid=(n // tm, nt + 1),
        in_specs=[pl.BlockSpec((d, tm), lambda i, j: (0, i), pipeline_mode=once),
                  pl.BlockSpec((heads * nk, tm), lambda i, j: (0, i), pipeline_mode=once),
                  pl.BlockSpec((heads * nk, tm), lambda i, j: (0, i), pipeline_mode=once),
                  pl.BlockSpec((heads, tm), lambda i, j: (0, i), pipeline_mode=once),
                  pl.BlockSpec((heads, tm), lambda i, j: (0, i), pipeline_mode=once),
                  pl.BlockSpec((te, d), lambda i, j: (jnp.minimum(j, nt - 1), 0)),
                  pl.BlockSpec((d, te), lambda i, j: (0, jnp.maximum(j - 1, 0)))],
        out_specs=pl.BlockSpec((d, tm), lambda i, j: (0, i), pipeline_mode=once),
        scratch_shapes=[pltpu.VMEM((te, tm), F32), pltpu.VMEM((te, tm), BF16)],
        compiler_params=_params(("parallel", "arbitrary"), vmem),
        name="peer_dense",
    )(xn_t, s1_t, s2_t, thr, cst, u, v_t)


def _final_kernel(x_ref, pt_ref, g_ref, o_ref):
    x = x_ref[...] + pt_ref[...].T
    r = lax.rsqrt(jnp.mean(x * x, axis=-1, keepdims=True) + EPS)
    o_ref[...] = (x * r) * g_ref[...]


def residual_rmsnorm_t(x, p_t, g, tm=256):
    n, d = x.shape
    tm = _tile(n, tm)
    return pl.pallas_call(
        _final_kernel,
        out_shape=jax.ShapeDtypeStruct((n, d), F32),
        grid=(n // tm,),
        in_specs=[pl.BlockSpec((tm, d), lambda i: (i, 0)),
                  pl.BlockSpec((d, tm), lambda i: (0, i)),
                  pl.BlockSpec((1, d), lambda i: (0, 0))],
        out_specs=pl.BlockSpec((tm, d), lambda i: (i, 0)),
        compiler_params=_params(("parallel",), 2 * 3 * tm * d * 4 + tm * d * 4),
        name="residual_rmsnorm_t",
    )(x, p_t, g.reshape(1, d))


def _residual_t_kernel(x_ref, pt_ref, o_ref):
    o_ref[...] = x_ref[...] + pt_ref[...].T


def residual_t(x, p_t, tm=256):
    n, d = x.shape
    tm = _tile(n, tm)
    return pl.pallas_call(
        _residual_t_kernel,
        out_shape=jax.ShapeDtypeStruct((n, d), F32),
        grid=(n // tm,),
        in_specs=[pl.BlockSpec((tm, d), lambda i: (i, 0)),
                  pl.BlockSpec((d, tm), lambda i: (0, i))],
        out_specs=pl.BlockSpec((tm, d), lambda i: (i, 0)),
        compiler_params=_params(("parallel",), 2 * 3 * tm * d * 4 + tm * d * 4),
        name="residual_t",
    )(x, p_t)


def _layer(x, hist, lw, chunk, want_v):
    b, t, d = x.shape
    n = b * t
    d_a = lw["conv_w"].shape[1]
    d_b = lw["ln_b_g"].shape[0]
    x2 = x.reshape(n, d)
    xn = rmsnorm_cast(x2, lw["norm_mix_g"])
    a = glu_proj(xn, lw["w_in"], d_a)
    uv = proj(xn, lw["w_in"], 2 * d_a, 2 * d_b)
    gates = proj(xn, lw["w_in"], 2 * d_a + 2 * d_b, 2 * d, bias=lw["b_gate"])
    act = conv_ln(a.reshape(b, t, d_a), hist, lw["conv_w"], lw["conv_b"], lw["ln_a_g"], lw["ln_a_b"]).reshape(n, d_a)
    p, *v = gmlp_mix(uv, lw["ln_b_g"], lw["ln_b_b"], lw["w_spatial"], lw["b_spatial"], chunk,
                     emit_v=want_v)
    m = merge_proj(act, p, gates, lw["w_a_out"], lw["w_b_out"])
    x1 = residual_proj(m, lw["w_o"], x2)
    xn_t, q_t = norm_query(x1, lw["norm_ffn_g"], lw["w_query_t"])
    s1_t, s2_t, thr, cst = peer_topk(q_t, lw["sub_keys"])
    peer_t = peer_dense(xn_t, s1_t, s2_t, thr, cst, lw["expert_u"], lw["expert_v_t"])
    return x1, peer_t, a.reshape(b, t, d_a), (v[0].reshape(b, t, d_b) if v else None)


def kernel(x_prompt, x_sample, cache_conv, norm_mix_g, w_in, b_gate, conv_w, conv_b, ln_a_g, ln_a_b, w_a_out, ln_b_g, ln_b_b, w_spatial, b_spatial, w_b_out, w_o, norm_ffn_g, w_query, sub_keys, expert_u, expert_v, norm_final_g):
    depth = w_in.shape[0]
    kw = conv_w.shape[1]
    d_a = conv_w.shape[2]
    gmlp_chunk = w_spatial.shape[2]
    xp, xs = x_prompt, x_sample
    bp, tp, d = xp.shape
    bs, ts, _ = xs.shape
    assert tp % gmlp_chunk == 0 and ts <= gmlp_chunk and ts % STREAM_CHUNK == 0
    assert tp >= kw - 1 and ts >= kw - 1
    conv_p, conv_s, v_s = [], [], []
    for l in range(depth):
        lw = dict(
            norm_mix_g=norm_mix_g[l], w_in=w_in[l].astype(BF16), b_gate=b_gate[l],
            conv_w=conv_w[l], conv_b=conv_b[l], ln_a_g=ln_a_g[l], ln_a_b=ln_a_b[l],
            w_a_out=w_a_out[l].astype(BF16), ln_b_g=ln_b_g[l], ln_b_b=ln_b_b[l],
            w_spatial=w_spatial[l], b_spatial=b_spatial[l], w_b_out=w_b_out[l].astype(BF16),
            w_o=w_o[l].astype(BF16), norm_ffn_g=norm_ffn_g[l],
            w_query_t=transpose_cast(w_query[l], BF16), sub_keys=sub_keys[l].astype(BF16),
            expert_u=expert_u[l].astype(BF16), expert_v_t=transpose_cast(expert_v[l], BF16))
        last = l == depth - 1
        hist_p = jnp.zeros((bp, HALO_ROWS, d_a), F32)
        hist_s = jnp.pad(cache_conv[l].astype(F32), ((0, 0), (HALO_ROWS - (kw - 1), 0), (0, 0)))
        outs = []
        for x, hist, chunk, want_v in ((xp, hist_p, gmlp_chunk, False), (xs, hist_s, ts, True)):
            b, t, _ = x.shape
            x1, peer_t, a, v = _layer(x, hist, lw, chunk, want_v)
            if last:
                y = residual_rmsnorm_t(x1, peer_t, norm_final_g)
            else:
                y = residual_t(x1, peer_t)
            outs.append((y.reshape(b, t, d), a, v))
        (xp, a_p, _), (xs, a_s, v_new) = outs
        conv_p.append(a_p[:, tp - (kw - 1):])
        conv_s.append(a_s[:, ts - (kw - 1):])
        v_s.append(v_new)
    return (xp, xs, jnp.stack(conv_p, axis=0), jnp.stack(conv_s, axis=0), jnp.stack(v_s, axis=0))
```
